```python
import math
import jax, jax.numpy as jnp
from jax import lax
import numpy as np

D_MODEL = 2048
BATCH = 8
SEQ = 2048
DEPTH = 1

ATTN_HEADS = 8
ATTN_HEAD_DIM = 64
ATTN_V_DIM = 2 * ATTN_HEAD_DIM
ATTN_QK_WIDTH = 2 * ATTN_HEADS * ATTN_HEAD_DIM
ATTN_WIDTH = ATTN_HEADS * ATTN_V_DIM
ROPE_THETA = 500000.0
ROT_DIM = ATTN_HEAD_DIM // 4
Q_BLOCK = 128
REC_EXPAND = 128
REC_WIDTH = D_MODEL // 2
REC_HEADS = REC_WIDTH // REC_EXPAND
REC_V_DIM = REC_WIDTH // REC_HEADS
REC_FORGET = REC_HEADS * REC_EXPAND
CHUNK = 64
D_FF = 5632
EPS = 1e-6
IN_SIZES = (ATTN_QK_WIDTH, ATTN_QK_WIDTH, ATTN_WIDTH,
            REC_FORGET, REC_FORGET, REC_WIDTH, REC_WIDTH,
            D_MODEL, D_MODEL)
IN_COLS = sum(IN_SIZES)

kernel_name = "hybrid_diffattn_hgrn2_macaron"


def rmsnorm(x, w):
    xf = x.astype(jnp.float32)
    y = xf * lax.rsqrt(jnp.mean(xf * xf, axis=-1, keepdims=True) + EPS)
    return (y * w.astype(jnp.float32)).astype(x.dtype)


def swiglu(h, w_in, w_out):
    gate, up = jnp.split(h @ w_in, 2, axis=-1)
    return (jax.nn.silu(gate) * up) @ w_out


def rope_partial(x, pos):
    inv_freq = ROPE_THETA ** (-jnp.arange(0, ROT_DIM, 2, dtype=jnp.float32) / ROT_DIM)
    ang = pos.astype(jnp.float32)[:, None] * inv_freq[None, :]
    ang = jnp.concatenate([ang, ang], axis=-1)[None, :, None, :]
    cos, sin = jnp.cos(ang), jnp.sin(ang)
    xr = x[..., :ROT_DIM].astype(jnp.float32)
    x1, x2 = jnp.split(xr, 2, axis=-1)
    rot = jnp.concatenate([-x2, x1], axis=-1)
    xr = (xr * cos + rot * sin).astype(x.dtype)
    return jnp.concatenate([xr, x[..., ROT_DIM:]], axis=-1)


def diff_attention(q, k, v, lq1, lk1, lq2, lk2, subln_w, layer):
    B, S, _ = q.shape
    H, d = ATTN_HEADS, ATTN_HEAD_DIM
    lambda_init = 0.8 - 0.6 * math.exp(-0.3 * layer)
    pos = jnp.arange(S)
    q = rope_partial(q.reshape(B, S, 2 * H, d), pos)
    k = rope_partial(k.reshape(B, S, 2 * H, d), pos)
    v = v.reshape(B, S, H, ATTN_V_DIM).transpose(0, 2, 1, 3)
    q = q.transpose(0, 2, 1, 3) * (d ** -0.5)
    k = k.transpose(0, 2, 1, 3)
    nb = S // Q_BLOCK
    qb = q.reshape(B, 2 * H, nb, Q_BLOCK, d).transpose(2, 0, 1, 3, 4)
    lam = (jnp.exp(jnp.sum(lq1.astype(jnp.float32) * lk1.astype(jnp.float32)))
           - jnp.exp(jnp.sum(lq2.astype(jnp.float32) * lk2.astype(jnp.float32)))
           + lambda_init)
    key_pos = jnp.arange(S)

    def block(args):
        i, qblk = args
        s = jnp.einsum('bhqd,bhkd->bhqk', qblk, k).astype(jnp.float32)
        q_pos = i * Q_BLOCK + jnp.arange(Q_BLOCK)
        s = jnp.where(q_pos[:, None] >= key_pos[None, :], s, -jnp.inf)
        p = jax.nn.softmax(s, axis=-1).reshape(B, H, 2, Q_BLOCK, S)
        w = p[:, :, 0] - lam * p[:, :, 1]
        return jnp.einsum('bhqk,bhkv->bhqv', w.astype(v.dtype), v)

    o = lax.map(block, (jnp.arange(nb), qb))
    o = o.transpose(1, 0, 3, 2, 4).reshape(B, S, H, ATTN_V_DIM)
    o = rmsnorm(o, subln_w) * (1.0 - lambda_init)
    return o.reshape(B, S, ATTN_WIDTH)


def hgrn2(q, fl, i, g, lb, gnorm_w):
    B, S, _ = q.shape
    H, K, V = REC_HEADS, REC_EXPAND, REC_V_DIM
    qf = jax.nn.silu(q.astype(jnp.float32)).reshape(B, S, H, K)
    lbh = lb.reshape(H, K)
    f = lbh + (1.0 - lbh) * jax.nn.sigmoid(fl.astype(jnp.float32).reshape(B, S, H, K))
    logf = jnp.log(f)
    kf = 1.0 - f
    vf = i.astype(jnp.float32).reshape(B, S, H, V)
    nc = S // CHUNK

    def to_chunks(t):
        return t.reshape(B, nc, CHUNK, H, -1).transpose(1, 0, 3, 2, 4)

    causal = jnp.tril(jnp.ones((CHUNK, CHUNK), dtype=bool))

    def step(state, inp):
        qc, kc, vc, gc = inp
        G = jnp.cumsum(gc, axis=2)
        inter = jnp.einsum('bhck,bhkv->bhcv', qc * jnp.exp(G), state)
        rel = G[:, :, :, None, :] - G[:, :, None, :, :]
        decay = jnp.exp(jnp.where(causal[:, :, None], rel, -jnp.inf))
        A = jnp.einsum('bhtk,bhsk,bhtsk->bhts', qc, kc, decay)
        intra = jnp.einsum('bhts,bhsv->bhtv', A, vc)
        G_last = G[:, :, -1:, :]
        new_state = (jnp.exp(G_last[:, :, 0, :])[..., None] * state
                     + jnp.einsum('bhsk,bhsv->bhkv', kc * jnp.exp(G_last - G), vc))
        return new_state, inter + intra

    s0 = jnp.zeros((B, H, K, V), jnp.float32)
    _, o = lax.scan(step, s0, (to_chunks(qf), to_chunks(kf), to_chunks(vf), to_chunks(logf)))
    o = o.transpose(1, 0, 3, 2, 4).reshape(B, S, H, V)
    o = rmsnorm(o, gnorm_w) * jax.nn.silu(g.astype(jnp.float32).reshape(B, S, H, V))
    return o.reshape(B, S, REC_WIDTH).astype(q.dtype)


def setup_inputs(seed: int = 0) -> dict:
    key = jax.random.key(seed)
    ks = jax.random.split(key, 24)
    f32 = jnp.float32

    def dense(k, shape, fan_in):
        return jax.random.normal(k, shape, f32) * (fan_in ** -0.5)

    def gain(k, shape):
        return 1.0 + 0.01 * jax.random.normal(k, shape, f32)

    return {
        "x": jax.random.normal(ks[0], (BATCH, SEQ, D_MODEL), f32),
        "ffn1_norm": gain(ks[1], (DEPTH, D_MODEL)),
        "ffn1_in": dense(ks[2], (DEPTH, D_MODEL, 2 * D_FF), D_MODEL),
        "ffn1_out": dense(ks[3], (DEPTH, D_FF, D_MODEL), D_FF),
        "mix_norm": gain(ks[4], (DEPTH, D_MODEL)),
        "w_in": dense(ks[5], (DEPTH, D_MODEL, IN_COLS), D_MODEL),
        "lambda_q1": 0.1 * jax.random.normal(ks[6], (DEPTH, ATTN_HEAD_DIM), f32),
        "lambda_k1": 0.1 * jax.random.normal(ks[7], (DEPTH, ATTN_HEAD_DIM), f32),
        "lambda_q2": 0.1 * jax.random.normal(ks[8], (DEPTH, ATTN_HEAD_DIM), f32),
        "lambda_k2": 0.1 * jax.random.normal(ks[9], (DEPTH, ATTN_HEAD_DIM), f32),
        "attn_subln": gain(ks[10], (DEPTH, ATTN_V_DIM)),
        "rec_lb_raw": 0.1 * jax.random.normal(ks[11], (DEPTH + 1, REC_FORGET), f32),
        "rec_gnorm": gain(ks[12], (DEPTH, REC_V_DIM)),
        "w_proj_attn": dense(ks[13], (DEPTH, ATTN_WIDTH, D_MODEL), ATTN_WIDTH),
        "w_proj_rec": dense(ks[14], (DEPTH, REC_WIDTH, D_MODEL), REC_WIDTH),
        "w_out": dense(ks[15], (DEPTH, D_MODEL, D_MODEL), D_MODEL),
        "ffn2_norm": gain(ks[16], (DEPTH, D_MODEL)),
        "ffn2_in": dense(ks[17], (DEPTH, D_MODEL, 2 * D_FF), D_MODEL),
        "ffn2_out": dense(ks[18], (DEPTH, D_FF, D_MODEL), D_FF),
        "final_norm": gain(ks[19], (D_MODEL,)),
    }


def reference(x, ffn1_norm, ffn1_in, ffn1_out, mix_norm, w_in, lambda_q1, lambda_k1,
              lambda_q2, lambda_k2, attn_subln, rec_lb_raw, rec_gnorm, w_proj_attn,
              w_proj_rec, w_out, ffn2_norm, ffn2_in, ffn2_out, final_norm):
    lower_bounds = jnp.cumsum(jax.nn.softmax(rec_lb_raw.astype(jnp.float32), axis=0), axis=0)
    split_at = [int(c) for c in np.cumsum(IN_SIZES)[:-1]]
    for l in range(DEPTH):
        x = x + 0.5 * swiglu(rmsnorm(x, ffn1_norm[l]), ffn1_in[l], ffn1_out[l])
        h = rmsnorm(x, mix_norm[l])
        q_a, k_a, v_a, q_r, f_r, i_r, g_r, gate_a, gate_b = jnp.split(h @ w_in[l], split_at, axis=-1)
        y_a = diff_attention(q_a, k_a, v_a, lambda_q1[l], lambda_k1[l], lambda_q2[l],
                             lambda_k2[l], attn_subln[l], l)
        y_r = hgrn2(q_r, f_r, i_r, g_r, lower_bounds[l], rec_gnorm[l])
        merged = (jax.nn.sigmoid(gate_a) * (y_a @ w_proj_attn[l])
                  + jax.nn.sigmoid(gate_b) * (y_r @ w_proj_rec[l]))
        x = x + merged @ w_out[l]
        x = x + 0.5 * swiglu(rmsnorm(x, ffn2_norm[l]), ffn2_in[l], ffn2_out[l])
    return rmsnorm(x, final_norm)
```

```python
import functools
import math

import jax
import jax.numpy as jnp
from jax import lax
from jax.experimental import pallas as pl
from jax.experimental.pallas import tpu as pltpu

F32 = jnp.float32
BF16 = jnp.bfloat16

EPS = 1e-6
ATTN_HEADS = 8
ATTN_HEAD_DIM = 64
ROPE_THETA = 500000.0
ROT_DIM = ATTN_HEAD_DIM // 4
HEAD_W = 128
REC_CHUNK = 64
LANES = 128
PROJ_TN = 1024
COL_QA, COL_KA, COL_VA, COL_QR, COL_FR, COL_IR, COL_GR, COL_GA, COL_GB = 0, 1, 2, 3, 4, 5, 6, 7, 9
VMEM_LIMIT = 52 * 1024 * 1024


def _rms(x, w):
    return x * lax.rsqrt(jnp.mean(x * x, axis=-1, keepdims=True) + EPS) * w


def _params(sem):
    return pltpu.CompilerParams(dimension_semantics=sem, vmem_limit_bytes=VMEM_LIMIT)


def _ffn_kernel(x_ref, nw_ref, wg_ref, wu_ref, wo_ref, fw_ref, o_ref, h_ref, *, nf, final_norm):
    f = pl.program_id(1)

    @pl.when(f == 0)
    def _():
        h_ref[...] = _rms(x_ref[...], nw_ref[...]).astype(BF16)

    h = h_ref[...]
    g = jnp.dot(h, wg_ref[...], preferred_element_type=F32)
    u = jnp.dot(h, wu_ref[...], preferred_element_type=F32)
    a = (g * jax.nn.sigmoid(g) * u).astype(BF16)
    p = jnp.dot(a, wo_ref[...], preferred_element_type=F32)

    @pl.when(f == 0)
    def _():
        o_ref[...] = p

    @pl.when(f > 0)
    def _():
        o_ref[...] += p

    @pl.when(f == nf - 1)
    def _():
        y = x_ref[...] + 0.5 * o_ref[...]
        if final_norm:
            y = _rms(y, fw_ref[...])
        o_ref[...] = y


def _ffn(x, norm_w, w_in, w_out, final_w, *, final_norm, tm=512, tf=512):
    t, d = x.shape
    ff = w_out.shape[0]
    tm, tf = min(tm, t), min(tf, ff)
    nf = ff // tf
    assert t % tm == 0 and ff % tf == 0
    return pl.pallas_call(
        functools.partial(_ffn_kernel, nf=nf, final_norm=final_norm),
        grid=(t // tm, nf),
        in_specs=[
            pl.BlockSpec((tm, d), lambda i, f: (i, 0)),
            pl.BlockSpec((1, d), lambda i, f: (0, 0)),
            pl.BlockSpec((d, tf), lambda i, f: (0, f)),
            pl.BlockSpec((d, tf), lambda i, f: (0, f + nf)),
            pl.BlockSpec((tf, d), lambda i, f: (f, 0)),
            pl.BlockSpec((1, d), lambda i, f: (0, 0)),
        ],
        out_specs=pl.BlockSpec((tm, d), lambda i, f: (i, 0)),
        out_shape=jax.ShapeDtypeStruct((t, d), F32),
        scratch_shapes=[pltpu.VMEM((tm, d), BF16)],
        compiler_params=_params(("parallel", "arbitrary")),
        name="ffn",
    )(x, norm_w.reshape(1, d), w_in, w_in, w_out, final_w.reshape(1, d))


def _proj_kernel(x_ref, nw_ref, w_ref, cos_ref, sina_ref, sinb_ref, o_ref, fl_ref, h_ref):
    j = pl.program_id(1)

    @pl.when(j == 0)
    def _():
        h_ref[...] = _rms(x_ref[...], nw_ref[...]).astype(BF16)

    r = jnp.dot(h_ref[...], w_ref[...], preferred_element_type=F32)
    n = r.shape[1]
    reps = n // LANES

    @pl.when(j <= COL_KA)
    def _():
        cos = jnp.concatenate([cos_ref[...]] * reps, axis=1)
        sina = jnp.concatenate([sina_ref[...]] * reps, axis=1)
        sinb = jnp.concatenate([sinb_ref[...]] * reps, axis=1)
        half = ROT_DIM // 2
        rot = r * cos + pltpu.roll(r, n - half, 1) * sina + pltpu.roll(r, half, 1) * sinb
        scale = jnp.where(j == COL_QA, ATTN_HEAD_DIM ** -0.5, 1.0).astype(F32)
        o_ref[...] = (rot * scale).astype(BF16)

    @pl.when(j > COL_KA)
    def _():
        o_ref[...] = r.astype(BF16)

    @pl.when(j == COL_FR)
    def _():
        fl_ref[...] = r


def _rope_tables(seq):
    half = ROT_DIM // 2
    inv_freq = ROPE_THETA ** (-jnp.arange(0, ROT_DIM, 2, dtype=F32) / ROT_DIM)
    ang = jnp.arange(seq).astype(F32)[:, None] * inv_freq[None, :]
    cos8, sin8 = jnp.cos(ang), jnp.sin(ang)
    pad = ATTN_HEAD_DIM - ROT_DIM
    ones = jnp.ones((seq, pad), F32)
    zeros = jnp.zeros((seq, pad), F32)
    z8 = jnp.zeros((seq, half), F32)
    cos = jnp.concatenate([cos8, cos8, ones], axis=1)
    sina = jnp.concatenate([-sin8, z8, zeros], axis=1)
    sinb = jnp.concatenate([z8, sin8, zeros], axis=1)
    rep = LANES // ATTN_HEAD_DIM
    return tuple(jnp.tile(a, (1, rep)) for a in (cos, sina, sinb))


def _proj(x, norm_w, w, seq, *, tm=512):
    t, d = x.shape
    n = w.shape[1]
    tm = min(tm, seq)
    tn = PROJ_TN
    assert t % tm == 0 and seq % tm == 0 and n % tn == 0
    spt = seq // tm
    cos, sina, sinb = _rope_tables(seq)
    tab_spec = pl.BlockSpec((tm, LANES), lambda i, j: (i % spt, 0))
    return pl.pallas_call(
        _proj_kernel,
        grid=(t // tm, n // tn),
        in_specs=[
            pl.BlockSpec((tm, d), lambda i, j: (i, 0)),
            pl.BlockSpec((1, d), lambda i, j: (0, 0)),
            pl.BlockSpec((d, tn), lambda i, j: (0, j)),
            tab_spec, tab_spec, tab_spec,
        ],
        out_specs=[
            pl.BlockSpec((tm, tn), lambda i, j: (i, j)),
            pl.BlockSpec((tm, tn), lambda i, j: (i, 0)),
        ],
        out_shape=[
            jax.ShapeDtypeStruct((t, n), BF16),
            jax.ShapeDtypeStruct((t, tn), F32),
        ],
        scratch_shapes=[pltpu.VMEM((tm, d), BF16)],
        compiler_params=_params(("parallel", "arbitrary")),
        name="proj",
    )(x, norm_w.reshape(1, d), w, cos, sina, sinb)


def _attn_kernel(q_ref, k_ref, v_ref, lq1_ref, lk1_ref, lq2_ref, lk2_ref, sw_ref, o_ref, *,
                 tq, lambda_init):
    qi = pl.program_id(2)
    hd = ATTN_HEAD_DIM
    q = q_ref[...]
    lane = lax.broadcasted_iota(jnp.int32, q.shape, 1)
    qs = (jnp.where(lane < hd, q, jnp.zeros_like(q)), jnp.where(lane >= hd, q, jnp.zeros_like(q)))

    def scores(kb):
        k = k_ref[pl.ds(pl.multiple_of(kb * tq, tq), tq), :]
        return tuple(
            lax.dot_general(qs[a], k, (((1,), (1,)), ((), ())), preferred_element_type=F32)
            for a in range(2))

    def update(kb, s, carry):
        v = v_ref[pl.ds(pl.multiple_of(kb * tq, tq), tq), :]
        out = []
        for a in range(2):
            m, l, acc = carry[a]
            m_new = jnp.maximum(m, jnp.max(s[a], axis=-1, keepdims=True))
            alpha = jnp.exp(m - m_new)
            p = jnp.exp(s[a] - m_new)
            l_new = alpha * l + jnp.sum(p, axis=-1, keepdims=True)
            acc_new = alpha * acc + jnp.dot(p.astype(BF16), v, preferred_element_type=F32)
            out.append((m_new, l_new, acc_new))
        return tuple(out)

    def body(kb, carry):
        return update(kb, scores(kb), carry)

    init = tuple((jnp.full((tq, 1), -jnp.inf, F32), jnp.zeros((tq, 1), F32),
                  jnp.zeros((tq, HEAD_W), F32)) for _ in range(2))
    carry = lax.fori_loop(0, qi, body, init)
    row = lax.broadcasted_iota(jnp.int32, (tq, tq), 0)
    col = lax.broadcasted_iota(jnp.int32, (tq, tq), 1)
    s = tuple(jnp.where(row >= col, sa, -jnp.inf) for sa in scores(qi))
    (_, l1, acc1), (_, l2, acc2) = update(qi, s, carry)

    lam = (jnp.exp(jnp.sum(lq1_ref[...] * lk1_ref[...], axis=-1, keepdims=True))
           - jnp.exp(jnp.sum(lq2_ref[...] * lk2_ref[...], axis=-1, keepdims=True))
           + lambda_init)
    o = acc1 / l1 - lam * (acc2 / l2)
    o_ref[...] = (_rms(o, sw_ref[...]) * (1.0 - lambda_init)).astype(BF16)


def _attn(proj, lq1, lk1, lq2, lk2, subln_w, batch, seq, *, layer=0, tq=256):
    t = proj.shape[0]
    tq = min(tq, seq)
    nq = seq // tq
    heads = ATTN_HEADS
    gpt = PROJ_TN // HEAD_W
    lambda_init = 0.8 - 0.6 * math.exp(-0.3 * layer)
    vec = lambda a: a.reshape(1, -1).astype(F32)
    small = lambda w: pl.BlockSpec((1, w), lambda b, h, i: (0, 0))
    return pl.pallas_call(
        functools.partial(_attn_kernel, tq=tq, lambda_init=lambda_init),
        grid=(batch, heads, nq),
        in_specs=[
            pl.BlockSpec((tq, HEAD_W), lambda b, h, i: (b * nq + i, COL_QA * gpt + h)),
            pl.BlockSpec((seq, HEAD_W), lambda b, h, i: (b, COL_KA * gpt + h)),
            pl.BlockSpec((seq, HEAD_W), lambda b, h, i: (b, COL_VA * gpt + h)),
            small(ATTN_HEAD_DIM), small(ATTN_HEAD_DIM), small(ATTN_HEAD_DIM), small(ATTN_HEAD_DIM),
            small(HEAD_W),
        ],
        out_specs=pl.BlockSpec((tq, HEAD_W), lambda b, h, i: (b * nq + i, h)),
        out_shape=jax.ShapeDtypeStruct((t, heads * HEAD_W), BF16),
        compiler_params=_params(("parallel", "parallel", "arbitrary")),
        name="attn",
    )(proj, proj, proj, vec(lq1), vec(lk1), vec(lq2), vec(lk2), vec(subln_w))


def _hgrn2_kernel(q_ref, fl_ref, i_ref, g_ref, lbraw_ref, gw_ref, o_ref, st_ref, *, nchunks, layer):
    c = REC_CHUNK
    raw = lbraw_ref[...]
    e = jnp.exp(raw - jnp.max(raw, axis=0, keepdims=True))
    lb = jnp.sum(e[0:layer + 1], axis=0, keepdims=True) / jnp.sum(e, axis=0, keepdims=True)
    gw = gw_ref[...]
    row = lax.broadcasted_iota(jnp.int32, (c, c), 0)
    col = lax.broadcasted_iota(jnp.int32, (c, c), 1)
    causal = row >= col
    tril = causal.astype(F32)
    st_ref[...] = jnp.zeros_like(st_ref)

    def body(ci, _):
        rows = pl.ds(pl.multiple_of(ci * c, c), c)
        q = q_ref[rows, :].astype(F32)
        qf = q * jax.nn.sigmoid(q)
        f = lb + (1.0 - lb) * jax.nn.sigmoid(fl_ref[rows, :])
        kf = 1.0 - f
        v = i_ref[rows, :]
        gcum = jnp.dot(tril, jnp.log(f), preferred_element_type=F32, precision=lax.Precision.HIGHEST)
        glast = gcum[c - 1:c, :]
        st = st_ref[...]
        qd = (qf * jnp.exp(gcum)).astype(BF16)
        inter = lax.dot_general(qd, st.astype(BF16), (((1,), (1,)), ((), ())),
                                preferred_element_type=F32)
        ku = (kf * jnp.exp(-gcum)).astype(BF16)
        a = lax.dot_general(qd, ku, (((1,), (1,)), ((), ())), preferred_element_type=F32)
        a = jnp.where(causal, a, 0.0)
        intra = jnp.dot(a.astype(BF16), v, preferred_element_type=F32)
        kd = (kf * jnp.exp(glast - gcum)).astype(BF16)
        upd = lax.dot_general(v, kd, (((0,), (0,)), ((), ())), preferred_element_type=F32)
        st_ref[...] = st * jnp.exp(glast) + upd
        o = _rms(inter + intra, gw)
        g = g_ref[rows, :].astype(F32)
        o_ref[rows, :] = (o * (g * jax.nn.sigmoid(g))).astype(BF16)
        return 0

    lax.fori_loop(0, nchunks, body, 0)


def _hgrn2(proj, fl, lb_raw, gnorm_w, batch, seq, *, layer=0):
    t = proj.shape[0]
    heads = fl.shape[1] // HEAD_W
    gpt = PROJ_TN // HEAD_W
    depth1 = lb_raw.shape[0]
    assert seq % REC_CHUNK == 0
    blk = lambda col: pl.BlockSpec((seq, HEAD_W), lambda b, h: (b, col * gpt + h))
    return pl.pallas_call(
        functools.partial(_hgrn2_kernel, nchunks=seq // REC_CHUNK, layer=layer),
        grid=(batch, heads),
        in_specs=[
            blk(COL_QR),
            pl.BlockSpec((seq, HEAD_W), lambda b, h: (b, h)),
            blk(COL_IR),
            blk(COL_GR),
            pl.BlockSpec((depth1, HEAD_W), lambda b, h: (0, h)),
            pl.BlockSpec((1, HEAD_W), lambda b, h: (0, 0)),
        ],
        out_specs=pl.BlockSpec((seq, HEAD_W), lambda b, h: (b, h)),
        out_shape=jax.ShapeDtypeStruct((t, heads * HEAD_W), BF16),
        scratch_shapes=[pltpu.VMEM((HEAD_W, HEAD_W), F32)],
        compiler_params=_params(("parallel", "parallel")),
        name="hgrn2",
    )(proj, fl, proj, proj, lb_raw.astype(F32), gnorm_w.reshape(1, -1).astype(F32))


def _merge_kernel(x_ref, ya_ref, yr_ref, ga0_ref, ga1_ref, gb0_ref, gb1_ref, wa_ref, wr_ref, wo_ref,
                  o_ref):
    pa = jnp.dot(ya_ref[...], wa_ref[...], preferred_element_type=F32)
    pr = jnp.dot(yr_ref[...], wr_ref[...], preferred_element_type=F32)
    ga = jnp.concatenate([ga0_ref[...], ga1_ref[...]], axis=1).astype(F32)
    gb = jnp.concatenate([gb0_ref[...], gb1_ref[...]], axis=1).astype(F32)
    merged = jax.nn.sigmoid(ga) * pa + jax.nn.sigmoid(gb) * pr
    o_ref[...] = x_ref[...] + jnp.dot(merged.astype(BF16), wo_ref[...], preferred_element_type=F32)


def _merge(x, ya, yr, proj, wa, wr, wo, *, tm=256):
    t, d = x.shape
    tm = min(tm, t)
    wa_k, wr_k = wa.shape[0], wr.shape[0]
    assert d == 2 * PROJ_TN
    const = lambda shape: pl.BlockSpec(shape, lambda i: (0, 0), pipeline_mode=pl.Buffered(1))
    gate = lambda col: pl.BlockSpec((tm, PROJ_TN), lambda i: (i, col))
    return pl.pallas_call(
        _merge_kernel,
        grid=(t // tm,),
        in_specs=[
            pl.BlockSpec((tm, d), lambda i: (i, 0)),
            pl.BlockSpec((tm, wa_k), lambda i: (i, 0)),
            pl.BlockSpec((tm, wr_k), lambda i: (i, 0)),
            gate(COL_GA), gate(COL_GA + 1), gate(COL_GB), gate(COL_GB + 1),
            const((wa_k, d)), const((wr_k, d)), const((d, d)),
        ],
        out_specs=pl.BlockSpec((tm, d), lambda i: (i, 0)),
        out_shape=jax.ShapeDtypeStruct((t, d), F32),
        compiler_params=_params(("parallel",)),
        name="merge",
    )(x, ya, yr, proj, proj, proj, proj, wa, wr, wo)


def kernel(x, ffn1_norm, ffn1_in, ffn1_out, mix_norm, w_in, lambda_q1, lambda_k1, lambda_q2,
           lambda_k2, attn_subln, rec_lb_raw, rec_gnorm, w_proj_attn, w_proj_rec, w_out,
           ffn2_norm, ffn2_in, ffn2_out, final_norm):
    batch, seq, d = x.shape
    depth = ffn1_in.shape[0]
    bf = lambda w: w.astype(BF16)
    xt = x.reshape(batch * seq, d)
    for l in range(depth):
        xt = _ffn(xt, ffn1_norm[l], bf(ffn1_in[l]), bf(ffn1_out[l]), final_norm, final_norm=False)
        proj, fl = _proj(xt, mix_norm[l], bf(w_in[l]), seq)
        ya = _attn(proj, lambda_q1[l], lambda_k1[l], lambda_q2[l], lambda_k2[l], attn_subln[l],
                   batch, seq, layer=l)
        yr = _hgrn2(proj, fl, rec_lb_raw, rec_gnorm[l], batch, seq, layer=l)
        xt = _merge(xt, ya, yr, proj, bf(w_proj_attn[l]), bf(w_proj_rec[l]), bf(w_out[l]))
        xt = _ffn(xt, ffn2_norm[l], bf(ffn2_in[l]), bf(ffn2_out[l]), final_norm,
                  final_norm=(l == depth - 1))
    return xt.reshape(batch, seq, d)
```

```python
import functools
import math

import jax
import jax.numpy as jnp
from jax import lax
from jax.experimental import pallas as pl
from jax.experimental.pallas import tpu as pltpu

F32 = jnp.float32
BF16 = jnp.bfloat16

EPS = 1e-6
ATTN_HEADS = 8
ATTN_HEAD_DIM = 64
ROPE_THETA = 500000.0
ROT_DIM = ATTN_HEAD_DIM // 4
HEAD_W = 128
REC_CHUNK = 64
REC_GROUP = 4
REC_SAFE_SPAN = 80.0
LANES = 128
PROJ_TN = 1024
COL_QA, COL_KA, COL_VA, COL_QR, COL_FR, COL_IR, COL_GR, COL_GA, COL_GB = 0, 1, 2, 3, 4, 5, 6, 7, 9
VMEM_LIMIT = 52 * 1024 * 1024


def _rms(x, w):
    return x * lax.rsqrt(jnp.mean(x * x, axis=-1, keepdims=True) + EPS) * w


def _params(sem):
    return pltpu.CompilerParams(dimension_semantics=sem, vmem_limit_bytes=VMEM_LIMIT)


def _ffn_kernel(x_ref, nw_ref, wg_ref, wu_ref, wo_ref, fw_ref, o_ref, h_ref, *, nf, final_norm):
    f = pl.program_id(1)

    @pl.when(f == 0)
    def _():
        x = x_ref[...]
        h_ref[...] = _rms(x, nw_ref[...]).astype(BF16)
        o_ref[...] = x

    h = h_ref[...]
    g = jnp.dot(h, wg_ref[...], preferred_element_type=F32)
    u = jnp.dot(h, wu_ref[...], preferred_element_type=F32)
    a = (g * jax.nn.sigmoid(g) * u * 0.5).astype(BF16)
    o_ref[...] += jnp.dot(a, wo_ref[...], preferred_element_type=F32)

    if final_norm:
        @pl.when(f == nf - 1)
        def _():
            o_ref[...] = _rms(o_ref[...], fw_ref[...])


def _ffn(x, norm_w, w_in, w_out, final_w, *, final_norm, tm=1024, tf=512):
    t, d = x.shape
    ff = w_out.shape[0]
    tm, tf = min(tm, t), min(tf, ff)
    nf = ff // tf
    assert t % tm == 0 and ff % tf == 0
    return pl.pallas_call(
        functools.partial(_ffn_kernel, nf=nf, final_norm=final_norm),
        grid=(t // tm, nf),
        in_specs=[
            pl.BlockSpec((tm, d), lambda i, f: (i, 0), pipeline_mode=pl.Buffered(1)),
            pl.BlockSpec((1, d), lambda i, f: (0, 0)),
            pl.BlockSpec((d, tf), lambda i, f: (0, f)),
            pl.BlockSpec((d, tf), lambda i, f: (0, f + nf)),
            pl.BlockSpec((tf, d), lambda i, f: (f, 0)),
            pl.BlockSpec((1, d), lambda i, f: (0, 0)),
        ],
        out_specs=pl.BlockSpec((tm, d), lambda i, f: (i, 0)),
        out_shape=jax.ShapeDtypeStruct((t, d), F32),
        scratch_shapes=[pltpu.VMEM((tm, d), BF16)],
        compiler_params=_params(("parallel", "arbitrary")),
        name="ffn",
    )(x, norm_w.reshape(1, d), w_in, w_in, w_out, final_w.reshape(1, d))


def _proj_kernel(x_ref, nw_ref, w_ref, cos_ref, sina_ref, sinb_ref, o_ref, fl_ref, h_ref):
    j = pl.program_id(1)

    @pl.when(j == 0)
    def _():
        h_ref[...] = _rms(x_ref[...], nw_ref[...]).astype(BF16)

    r = jnp.dot(h_ref[...], w_ref[...], preferred_element_type=F32)
    n = r.shape[1]
    reps = n // LANES
    o_ref[...] = r.astype(BF16)

    @pl.when(j <= COL_KA)
    def _():
        cos = jnp.concatenate([cos_ref[...]] * reps, axis=1)
        sina = jnp.concatenate([sina_ref[...]] * reps, axis=1)
        sinb = jnp.concatenate([sinb_ref[...]] * reps, axis=1)
        half = ROT_DIM // 2
        rot = r * cos + pltpu.roll(r, n - half, 1) * sina + pltpu.roll(r, half, 1) * sinb
        scale = jnp.where(j == COL_QA, ATTN_HEAD_DIM ** -0.5, 1.0).astype(F32)
        o_ref[...] = (rot * scale).astype(BF16)

    @pl.when(j == COL_FR)
    def _():
        fl_ref[...] = r


def _rope_tables(seq):
    half = ROT_DIM // 2
    inv_freq = ROPE_THETA ** (-jnp.arange(0, ROT_DIM, 2, dtype=F32) / ROT_DIM)
    ang = jnp.arange(seq).astype(F32)[:, None] * inv_freq[None, :]
    cos8, sin8 = jnp.cos(ang), jnp.sin(ang)
    pad = ATTN_HEAD_DIM - ROT_DIM
    ones = jnp.ones((seq, pad), F32)
    zeros = jnp.zeros((seq, pad), F32)
    z8 = jnp.zeros((seq, half), F32)
    cos = jnp.concatenate([cos8, cos8, ones], axis=1)
    sina = jnp.concatenate([-sin8, z8, zeros], axis=1)
    sinb = jnp.concatenate([z8, sin8, zeros], axis=1)
    rep = LANES // ATTN_HEAD_DIM
    return tuple(jnp.tile(a, (1, rep)) for a in (cos, sina, sinb))


def _proj(x, norm_w, w, seq, *, tm=512):
    t, d = x.shape
    n = w.shape[1]
    tm = min(tm, seq)
    tn = PROJ_TN
    assert t % tm == 0 and seq % tm == 0 and n % tn == 0
    spt = seq // tm
    cos, sina, sinb = _rope_tables(seq)
    tab_spec = pl.BlockSpec((tm, LANES), lambda i, j: (i % spt, 0))
    return pl.pallas_call(
        _proj_kernel,
        grid=(t // tm, n // tn),
        in_specs=[
            pl.BlockSpec((tm, d), lambda i, j: (i, 0)),
            pl.BlockSpec((1, d), lambda i, j: (0, 0)),
            pl.BlockSpec((d, tn), lambda i, j: (0, j)),
            tab_spec, tab_spec, tab_spec,
        ],
        out_specs=[
            pl.BlockSpec((tm, tn), lambda i, j: (i, j)),
            pl.BlockSpec((tm, tn), lambda i, j: (i, 0)),
        ],
        out_shape=[
            jax.ShapeDtypeStruct((t, n), BF16),
            jax.ShapeDtypeStruct((t, tn), F32),
        ],
        scratch_shapes=[pltpu.VMEM((tm, d), BF16)],
        compiler_params=_params(("parallel", "arbitrary")),
        name="proj",
    )(x, norm_w.reshape(1, d), w, cos, sina, sinb)


def _attn_kernel(q_ref, k_ref, v_ref, lq1_ref, lk1_ref, lq2_ref, lk2_ref, sw_ref, o_ref,
                 vx_ref, *, tq, lambda_init):
    hd = ATTN_HEAD_DIM
    w = HEAD_W
    seq = q_ref.shape[0]
    nt = (((1,), (1,)), ((), ()))
    vx_ref[:, 0:w] = v_ref[...]
    vx_ref[:, w:2 * w] = jnp.ones(v_ref.shape, BF16)
    lam = (jnp.exp(jnp.sum(lq1_ref[...] * lk1_ref[...], axis=-1, keepdims=True))
           - jnp.exp(jnp.sum(lq2_ref[...] * lk2_ref[...], axis=-1, keepdims=True))
           + lambda_init)
    lane = lax.broadcasted_iota(jnp.int32, (tq, w), 1)
    row = lax.broadcasted_iota(jnp.int32, (tq, tq), 0)
    col = lax.broadcasted_iota(jnp.int32, (tq, tq), 1)
    keep = row >= col
    keep2 = jnp.concatenate([keep, keep], axis=0)

    for qi in range(seq // tq):
        q = q_ref[qi * tq:(qi + 1) * tq, :]
        zero = jnp.zeros_like(q)
        qs = jnp.concatenate([jnp.where(lane < hd, q, zero), jnp.where(lane >= hd, q, zero)], axis=0)
        m = jnp.full((2 * tq, w), -jnp.inf, F32)
        acc = jnp.zeros((2 * tq, 2 * w), F32)
        for kb in range(qi + 1):
            s = lax.dot_general(qs, k_ref[kb * tq:(kb + 1) * tq, :], nt,
                                preferred_element_type=F32)
            if kb == qi:
                s = jnp.where(keep2, s, -jnp.inf)
            m_new = jnp.maximum(m, jnp.max(s, axis=-1, keepdims=True))
            alpha = jnp.exp(m - m_new)
            p = jnp.exp(s - jnp.concatenate([m_new] * (tq // w), axis=1))
            acc = (jnp.concatenate([alpha, alpha], axis=1) * acc
                   + jnp.dot(p.astype(BF16), vx_ref[kb * tq:(kb + 1) * tq, :],
                             preferred_element_type=F32))
            m = m_new
        o = acc[:, 0:w] / acc[:, w:2 * w]
        o = o[0:tq, :] - lam * o[tq:2 * tq, :]
        o_ref[qi * tq:(qi + 1) * tq, :] = (_rms(o, sw_ref[...]) * (1.0 - lambda_init)).astype(BF16)


def _attn(proj, lq1, lk1, lq2, lk2, subln_w, batch, seq, *, layer=0, tq=512):
    t = proj.shape[0]
    tq = min(tq, seq)
    assert seq % tq == 0 and tq % HEAD_W == 0
    heads = ATTN_HEADS
    gpt = PROJ_TN // HEAD_W
    lambda_init = 0.8 - 0.6 * math.exp(-0.3 * layer)
    vec = lambda a: a.reshape(1, -1).astype(F32)
    small = lambda w: pl.BlockSpec((1, w), lambda b, h: (0, 0))
    blk = lambda col: pl.BlockSpec((seq, HEAD_W), lambda b, h: (b, col * gpt + h))
    return pl.pallas_call(
        functools.partial(_attn_kernel, tq=tq, lambda_init=lambda_init),
        grid=(batch, heads),
        in_specs=[
            blk(COL_QA), blk(COL_KA), blk(COL_VA),
            small(ATTN_HEAD_DIM), small(ATTN_HEAD_DIM), small(ATTN_HEAD_DIM), small(ATTN_HEAD_DIM),
            small(HEAD_W),
        ],
        out_specs=pl.BlockSpec((seq, HEAD_W), lambda b, h: (b, h)),
        out_shape=jax.ShapeDtypeStruct((t, heads * HEAD_W), BF16),
        scratch_shapes=[pltpu.VMEM((seq, 2 * HEAD_W), BF16)],
        compiler_params=_params(("parallel", "parallel")),
        name="attn",
    )(proj, proj, proj, vec(lq1), vec(lk1), vec(lq2), vec(lk2), vec(subln_w))


def _hgrn2_kernel(q_ref, fl_ref, i_ref, g_ref, lbraw_ref, gw_ref, o_ref,
                  g_s, kf_s, qd_s, intra_s, upd_s, eg_s, st_s, *, nchunks, layer):
    c = REC_CHUNK
    nt = (((1,), (1,)), ((), ()))
    tn = (((0,), (0,)), ((), ()))
    raw = lbraw_ref[...]
    e = jnp.exp(raw - jnp.max(raw, axis=0, keepdims=True))
    lb = jnp.sum(e[0:layer + 1], axis=0, keepdims=True) / jnp.sum(e, axis=0, keepdims=True)
    gw = gw_ref[...]
    seq = nchunks * c
    grp = REC_GROUP
    gr = grp * c
    ngroups = nchunks // grp

    f = lb + (1.0 - lb) * jax.nn.sigmoid(fl_ref[...])
    kf_s[...] = 1.0 - f
    g = jnp.log(f)
    pos = lax.broadcasted_iota(jnp.int32, (seq, HEAD_W), 0) % c
    shift = 1
    while shift < c:
        g = g + jnp.where(pos >= shift, pltpu.roll(g, shift, 0), 0.0)
        shift *= 2
    g_s[...] = g
    factorable = jnp.min(g) >= -REC_SAFE_SPAN

    def chunk_last(gcum):
        n = gcum.shape[0] // c
        return jnp.concatenate(
            [jnp.broadcast_to(gcum[(j + 1) * c - 1:(j + 1) * c, :], (c, HEAD_W)) for j in range(n)],
            axis=0)

    for gi in range(ngroups):
        rows = slice(gi * gr, (gi + 1) * gr)
        q = q_ref[rows, :].astype(F32)
        gcum = g_s[rows, :]
        qd_s[rows, :] = (q * jax.nn.sigmoid(q) * jnp.exp(gcum)).astype(BF16)
        glast = chunk_last(gcum)
        kd = (kf_s[rows, :] * jnp.exp(glast - gcum)).astype(BF16)
        v = i_ref[rows, :]
        for j in range(grp):
            ci = gi * grp + j
            sub = slice(j * c, (j + 1) * c)
            upd_s[ci] = lax.dot_general(v[sub, :], kd[sub, :], tn, preferred_element_type=F32)
            eg_s[ci] = jnp.exp(glast[j * c:j * c + 8, :])

    @pl.when(factorable)
    def _():
        row = lax.broadcasted_iota(jnp.int32, (gr, gr), 0)
        col = lax.broadcasted_iota(jnp.int32, (gr, gr), 1)
        keep = (row >= col) & ((row // c) == (col // c))
        for gi in range(ngroups):
            rows = slice(gi * gr, (gi + 1) * gr)
            ku = (kf_s[rows, :] * jnp.exp(-g_s[rows, :])).astype(BF16)
            a = lax.dot_general(qd_s[rows, :], ku, nt, preferred_element_type=F32)
            a = jnp.where(keep, a, 0.0).astype(BF16)
            intra_s[rows, :] = jnp.dot(a, i_ref[rows, :], preferred_element_type=F32)

    @pl.when(jnp.logical_not(factorable))
    def _():
        row = lax.broadcasted_iota(jnp.int32, (c, c), 0)
        col = lax.broadcasted_iota(jnp.int32, (c, c), 1)

        def chunk(ci, _):
            rows = pl.ds(pl.multiple_of(ci * c, c), c)
            q = q_ref[rows, :].astype(F32)
            qf = q * jax.nn.sigmoid(q)
            gcum = g_s[rows, :]

            def column(s, a):
                src = pl.ds(ci * c + s, 1)
                term = qf * kf_s[src, :] * jnp.exp(jnp.minimum(gcum - g_s[src, :], 0.0))
                return jnp.where(col == s, jnp.sum(term, axis=1, keepdims=True), a)

            a = lax.fori_loop(0, c, column, jnp.zeros((c, c), F32))
            a = jnp.where(row >= col, a, 0.0).astype(BF16)
            intra_s[rows, :] = jnp.dot(a, i_ref[rows, :], preferred_element_type=F32)
            return 0

        lax.fori_loop(0, nchunks, chunk, 0)

    st = jnp.zeros((HEAD_W, HEAD_W), F32)
    for ci in range(nchunks):
        st_s[ci] = st.astype(BF16)
        st = st * jnp.broadcast_to(eg_s[ci][0:1, :], st.shape) + upd_s[ci]

    for gi in range(ngroups):
        rows = slice(gi * gr, (gi + 1) * gr)
        inter = jnp.concatenate(
            [lax.dot_general(qd_s[(gi * grp + j) * c:(gi * grp + j + 1) * c, :], st_s[gi * grp + j], nt,
                             preferred_element_type=F32) for j in range(grp)], axis=0)
        o = _rms(inter + intra_s[rows, :], gw)
        gate = g_ref[rows, :].astype(F32)
        o_ref[rows, :] = (o * (gate * jax.nn.sigmoid(gate))).astype(BF16)


def _hgrn2(proj, fl, lb_raw, gnorm_w, batch, seq, *, layer=0):
    t = proj.shape[0]
    heads = fl.shape[1] // HEAD_W
    gpt = PROJ_TN // HEAD_W
    depth1 = lb_raw.shape[0]
    assert seq % (REC_CHUNK * REC_GROUP) == 0
    nchunks = seq // REC_CHUNK
    blk = lambda col: pl.BlockSpec((seq, HEAD_W), lambda b, h: (b, col * gpt + h))
    return pl.pallas_call(
        functools.partial(_hgrn2_kernel, nchunks=nchunks, layer=layer),
        grid=(batch, heads),
        in_specs=[
            blk(COL_QR),
            pl.BlockSpec((seq, HEAD_W), lambda b, h: (b, h)),
            blk(COL_IR),
            blk(COL_GR),
            pl.BlockSpec((depth1, HEAD_W), lambda b, h: (0, h)),
            pl.BlockSpec((1, HEAD_W), lambda b, h: (0, 0)),
        ],
        out_specs=pl.BlockSpec((seq, HEAD_W), lambda b, h: (b, h)),
        out_shape=jax.ShapeDtypeStruct((t, heads * HEAD_W), BF16),
        scratch_shapes=[
            pltpu.VMEM((seq, HEAD_W), F32),
            pltpu.VMEM((seq, HEAD_W), F32),
            pltpu.VMEM((seq, HEAD_W), BF16),
            pltpu.VMEM((seq, HEAD_W), F32),
            pltpu.VMEM((nchunks, HEAD_W, HEAD_W), F32),
            pltpu.VMEM((nchunks, 8, HEAD_W), F32),
            pltpu.VMEM((nchunks, HEAD_W, HEAD_W), BF16),
        ],
        compiler_params=_params(("parallel", "parallel")),
        name="hgrn2",
    )(proj, fl, proj, proj, lb_raw.astype(F32), gnorm_w.reshape(1, -1).astype(F32))


def _merge_kernel(x_ref, ya_ref, yr_ref, ga0_ref, ga1_ref, gb0_ref, gb1_ref, wa_ref, wr_ref, wo_ref,
                  o_ref):
    pa = jnp.dot(ya_ref[...], wa_ref[...], preferred_element_type=F32)
    pr = jnp.dot(yr_ref[...], wr_ref[...], preferred_element_type=F32)
    ga = jnp.concatenate([ga0_ref[...], ga1_ref[...]], axis=1).astype(F32)
    gb = jnp.concatenate([gb0_ref[...], gb1_ref[...]], axis=1).astype(F32)
    merged = jax.nn.sigmoid(ga) * pa + jax.nn.sigmoid(gb) * pr
    o_ref[...] = x_ref[...] + jnp.dot(merged.astype(BF16), wo_ref[...], preferred_element_type=F32)


def _merge(x, ya, yr, proj, wa, wr, wo, *, tm=256):
    t, d = x.shape
    tm = min(tm, t)
    wa_k, wr_k = wa.shape[0], wr.shape[0]
    assert d == 2 * PROJ_TN
    const = lambda shape: pl.BlockSpec(shape, lambda i: (0, 0), pipeline_mode=pl.Buffered(1))
    gate = lambda col: pl.BlockSpec((tm, PROJ_TN), lambda i: (i, col))
    return pl.pallas_call(
        _merge_kernel,
        grid=(t // tm,),
        in_specs=[
            pl.BlockSpec((tm, d), lambda i: (i, 0)),
            pl.BlockSpec((tm, wa_k), lambda i: (i, 0)),
            pl.BlockSpec((tm, wr_k), lambda i: (i, 0)),
            gate(COL_GA), gate(COL_GA + 1), gate(COL_GB), gate(COL_GB + 1),
            const((wa_k, d)), const((wr_k, d)), const((d, d)),
        ],
        out_specs=pl.BlockSpec((tm, d), lambda i: (i, 0)),
        out_shape=jax.ShapeDtypeStruct((t, d), F32),
        compiler_params=_params(("parallel",)),
        name="merge",
    )(x, ya, yr, proj, proj, proj, proj, wa, wr, wo)


def kernel(x, ffn1_norm, ffn1_in, ffn1_out, mix_norm, w_in, lambda_q1, lambda_k1, lambda_q2,
           lambda_k2, attn_subln, rec_lb_raw, rec_gnorm, w_proj_attn, w_proj_rec, w_out,
           ffn2_norm, ffn2_in, ffn2_out, final_norm):
    batch, seq, d = x.shape
    depth = ffn1_in.shape[0]
    bf = lambda w: w.astype(BF16)
    xt = x.reshape(batch * seq, d)
    for l in range(depth):
        xt = _ffn(xt, ffn1_norm[l], bf(ffn1_in[l]), bf(ffn1_out[l]), final_norm, final_norm=False)
        proj, fl = _proj(xt, mix_norm[l], bf(w_in[l]), seq)
        ya = _attn(proj, lambda_q1[l], lambda_k1[l], lambda_q2[l], lambda_k2[l], attn_subln[l],
                   batch, seq, layer=l)
        yr = _hgrn2(proj, fl, rec_lb_raw, rec_gnorm[l], batch, seq, layer=l)
        xt = _merge(xt, ya, yr, proj, bf(w_proj_attn[l]), bf(w_proj_rec[l]), bf(w_out[l]))
        xt = _ffn(xt, ffn2_norm[l], bf(ffn2_in[l]), bf(ffn2_out[l]), final_norm,
                  final_norm=(l == depth - 1))
    return xt.reshape(batch, seq, d)
```

```python
import functools
import math

import jax
import jax.numpy as jnp
from jax import lax
from jax.experimental import pallas as pl
from jax.experimental.pallas import tpu as pltpu

F32 = jnp.float32
BF16 = jnp.bfloat16

EPS = 1e-6
ATTN_HEADS = 8
ATTN_HEAD_DIM = 64
ROPE_THETA = 500000.0
ROT_DIM = ATTN_HEAD_DIM // 4
HEAD_W = 128
REC_CHUNK = 64
REC_GROUP = 4
REC_SAFE_SPAN = 80.0
LANES = 128
SUBLANES = 8
PROJ_TN = 1024
COL_QA, COL_KA, COL_VA, COL_QR, COL_FR, COL_IR, COL_GR, COL_GA, COL_GB = 0, 1, 2, 3, 4, 5, 6, 7, 9
VMEM_LIMIT = 52 * 1024 * 1024


def _rms(x, w):
    return x * lax.rsqrt(jnp.mean(x * x, axis=-1, keepdims=True) + EPS) * w


def _params(sem):
    return pltpu.CompilerParams(dimension_semantics=sem, vmem_limit_bytes=VMEM_LIMIT)


def _ffn_kernel(x_ref, nw_ref, wg_ref, wu_ref, wo_ref, fw_ref, o_ref, h_ref, *, nf, final_norm):
    f = pl.program_id(1)

    @pl.when(f == 0)
    def _():
        x = x_ref[...]
        h_ref[...] = _rms(x, nw_ref[...]).astype(BF16)
        o_ref[...] = x

    h = h_ref[...]
    g = jnp.dot(h, wg_ref[...], preferred_element_type=F32)
    u = jnp.dot(h, wu_ref[...], preferred_element_type=F32)
    a = (g * jax.nn.sigmoid(g) * u * 0.5).astype(BF16)
    o_ref[...] += jnp.dot(a, wo_ref[...], preferred_element_type=F32)

    if final_norm:
        @pl.when(f == nf - 1)
        def _():
            o_ref[...] = _rms(o_ref[...], fw_ref[...])


def _ffn(x, norm_w, w_in, w_out, final_w, *, final_norm, tm=1024, tf=512):
    t, d = x.shape
    ff = w_out.shape[0]
    tm, tf = min(tm, t), min(tf, ff)
    nf = ff // tf
    assert t % tm == 0 and ff % tf == 0
    return pl.pallas_call(
        functools.partial(_ffn_kernel, nf=nf, final_norm=final_norm),
        grid=(t // tm, nf),
        in_specs=[
            pl.BlockSpec((tm, d), lambda i, f: (i, 0), pipeline_mode=pl.Buffered(1)),
            pl.BlockSpec((1, d), lambda i, f: (0, 0)),
            pl.BlockSpec((d, tf), lambda i, f: (0, f)),
            pl.BlockSpec((d, tf), lambda i, f: (0, f + nf)),
            pl.BlockSpec((tf, d), lambda i, f: (f, 0)),
            pl.BlockSpec((1, d), lambda i, f: (0, 0)),
        ],
        out_specs=pl.BlockSpec((tm, d), lambda i, f: (i, 0)),
        out_shape=jax.ShapeDtypeStruct((t, d), F32),
        scratch_shapes=[pltpu.VMEM((tm, d), BF16)],
        compiler_params=_params(("parallel", "arbitrary")),
        name="ffn",
    )(x, norm_w.reshape(1, d), w_in, w_in, w_out, final_w.reshape(1, d))


def _proj_kernel(x_ref, nw_ref, w_ref, cos_ref, sina_ref, sinb_ref, o_ref, fl_ref, h_ref):
    j = pl.program_id(1)

    @pl.when(j == 0)
    def _():
        h_ref[...] = _rms(x_ref[...], nw_ref[...]).astype(BF16)

    r = jnp.dot(h_ref[...], w_ref[...], preferred_element_type=F32)
    o_ref[...] = r.astype(BF16)

    @pl.when(j <= COL_KA)
    def _():
        half = ROT_DIM // 2
        scale = jnp.where(j == COL_QA, ATTN_HEAD_DIM ** -0.5 * math.log2(math.e), 1.0).astype(F32)
        cos = cos_ref[...] * scale
        sina = sina_ref[...] * scale
        sinb = sinb_ref[...] * scale
        for c0 in range(0, r.shape[1], LANES):
            rc = r[:, c0:c0 + LANES]
            rot = rc * cos + pltpu.roll(rc, LANES - half, 1) * sina + pltpu.roll(rc, half, 1) * sinb
            o_ref[:, c0:c0 + LANES] = rot.astype(BF16)

    @pl.when(j == COL_FR)
    def _():
        fl_ref[...] = r


def _rope_tables(seq):
    half = ROT_DIM // 2
    inv_freq = ROPE_THETA ** (-jnp.arange(0, ROT_DIM, 2, dtype=F32) / ROT_DIM)
    ang = jnp.arange(seq).astype(F32)[:, None] * inv_freq[None, :]
    cos8, sin8 = jnp.cos(ang), jnp.sin(ang)
    pad = ATTN_HEAD_DIM - ROT_DIM
    ones = jnp.ones((seq, pad), F32)
    zeros = jnp.zeros((seq, pad), F32)
    z8 = jnp.zeros((seq, half), F32)
    cos = jnp.concatenate([cos8, cos8, ones], axis=1)
    sina = jnp.concatenate([-sin8, z8, zeros], axis=1)
    sinb = jnp.concatenate([z8, sin8, zeros], axis=1)
    rep = LANES // ATTN_HEAD_DIM
    return tuple(jnp.tile(a, (1, rep)) for a in (cos, sina, sinb))


def _proj(x, norm_w, w, seq, *, tm=1024):
    t, d = x.shape
    n = w.shape[1]
    tm = min(tm, seq)
    tn = PROJ_TN
    assert t % tm == 0 and seq % tm == 0 and n % tn == 0
    spt = seq // tm
    cos, sina, sinb = _rope_tables(seq)
    tab_spec = pl.BlockSpec((tm, LANES), lambda i, j: (i % spt, 0))
    return pl.pallas_call(
        _proj_kernel,
        grid=(t // tm, n // tn),
        in_specs=[
            pl.BlockSpec((tm, d), lambda i, j: (i, 0), pipeline_mode=pl.Buffered(1)),
            pl.BlockSpec((1, d), lambda i, j: (0, 0)),
            pl.BlockSpec((d, tn), lambda i, j: (0, j)),
            tab_spec, tab_spec, tab_spec,
        ],
        out_specs=[
            pl.BlockSpec((tm, tn), lambda i, j: (i, j)),
            pl.BlockSpec((tm, tn), lambda i, j: (i, 0)),
        ],
        out_shape=[
            jax.ShapeDtypeStruct((t, n), BF16),
            jax.ShapeDtypeStruct((t, tn), F32),
        ],
        scratch_shapes=[pltpu.VMEM((tm, d), BF16)],
        compiler_params=_params(("parallel", "arbitrary")),
        name="proj",
    )(x, norm_w.reshape(1, d), w, cos, sina, sinb)


def _attn_kernel(q_ref, k_ref, v_ref, lq1_ref, lk1_ref, lq2_ref, lk2_ref, sw_ref, o_ref,
                 vx_ref, *, tq, lambda_init):
    hd = ATTN_HEAD_DIM
    w = HEAD_W
    hq = tq // 2
    seq = q_ref.shape[0]
    nt = (((1,), (1,)), ((), ()))
    vx_ref[:, 0:w] = v_ref[...]
    vx_ref[:, w:2 * w] = jnp.ones(v_ref.shape, BF16)
    lam = (jnp.exp(jnp.sum(lq1_ref[...] * lk1_ref[...], axis=-1, keepdims=True))
           - jnp.exp(jnp.sum(lq2_ref[...] * lk2_ref[...], axis=-1, keepdims=True))
           + lambda_init)
    lane = lax.broadcasted_iota(jnp.int32, (hq, w), 1)
    row = lax.broadcasted_iota(jnp.int32, (hq, hq), 0)
    col = lax.broadcasted_iota(jnp.int32, (hq, hq), 1)
    tri = row >= col
    tri2 = jnp.concatenate([tri, tri], axis=0)
    tri2_all = jnp.concatenate([tri2, jnp.ones((2 * hq, hq), jnp.bool_)], axis=0)

    def update(m, acc, s, vx):
        m_new = jnp.maximum(m, jnp.max(s, axis=-1, keepdims=True))
        alpha = jnp.exp2(m - m_new)
        p = jnp.exp2(s - jnp.concatenate([m_new] * (s.shape[1] // w), axis=1))
        acc = (jnp.concatenate([alpha, alpha], axis=1) * acc
               + jnp.dot(p.astype(BF16), vx, preferred_element_type=F32))
        return m_new, acc

    for qi in range(seq // tq):
        parts = []
        for r0 in (qi * tq, qi * tq + hq):
            q = q_ref[r0:r0 + hq, :]
            zero = jnp.zeros_like(q)
            parts += [jnp.where(lane < hd, q, zero), jnp.where(lane >= hd, q, zero)]
        qs = jnp.concatenate(parts, axis=0)
        m = jnp.full((2 * tq, w), -jnp.inf, F32)
        acc = jnp.zeros((2 * tq, 2 * w), F32)
        for kb in range(qi):
            keys = slice(kb * tq, (kb + 1) * tq)
            s = lax.dot_general(qs, k_ref[keys, :], nt, preferred_element_type=F32)
            m, acc = update(m, acc, s, vx_ref[keys, :])
        keys = slice(qi * tq, qi * tq + hq)
        s = lax.dot_general(qs, k_ref[keys, :], nt, preferred_element_type=F32)
        m, acc = update(m, acc, jnp.where(tri2_all, s, -jnp.inf), vx_ref[keys, :])
        keys = slice(qi * tq + hq, (qi + 1) * tq)
        s = lax.dot_general(qs[tq:, :], k_ref[keys, :], nt, preferred_element_type=F32)
        m_hi, acc_hi = update(m[tq:, :], acc[tq:, :], jnp.where(tri2, s, -jnp.inf), vx_ref[keys, :])
        acc = jnp.concatenate([acc[:tq, :], acc_hi], axis=0)
        o = acc[:, 0:w] / acc[:, w:2 * w]
        o = jnp.concatenate([o[0:hq, :] - lam * o[hq:tq, :],
                             o[tq:tq + hq, :] - lam * o[tq + hq:2 * tq, :]], axis=0)
        o_ref[qi * tq:(qi + 1) * tq, :] = (_rms(o, sw_ref[...]) * (1.0 - lambda_init)).astype(BF16)


def _attn(proj, lq1, lk1, lq2, lk2, subln_w, batch, seq, *, layer=0, tq=512):
    t = proj.shape[0]
    tq = min(tq, seq)
    assert seq % tq == 0 and tq % HEAD_W == 0
    heads = ATTN_HEADS
    gpt = PROJ_TN // HEAD_W
    lambda_init = 0.8 - 0.6 * math.exp(-0.3 * layer)
    vec = lambda a: a.reshape(1, -1).astype(F32)
    small = lambda w: pl.BlockSpec((1, w), lambda b, h: (0, 0))
    blk = lambda col: pl.BlockSpec((seq, HEAD_W), lambda b, h: (b, col * gpt + h))
    return pl.pallas_call(
        functools.partial(_attn_kernel, tq=tq, lambda_init=lambda_init),
        grid=(batch, heads),
        in_specs=[
            blk(COL_QA), blk(COL_KA), blk(COL_VA),
            small(ATTN_HEAD_DIM), small(ATTN_HEAD_DIM), small(ATTN_HEAD_DIM), small(ATTN_HEAD_DIM),
            small(HEAD_W),
        ],
        out_specs=pl.BlockSpec((seq, HEAD_W), lambda b, h: (b, h)),
        out_shape=jax.ShapeDtypeStruct((t, heads * HEAD_W), BF16),
        scratch_shapes=[pltpu.VMEM((seq, 2 * HEAD_W), BF16)],
        compiler_params=_params(("parallel", "parallel")),
        name="attn",
    )(proj, proj, proj, vec(lq1), vec(lk1), vec(lq2), vec(lk2), vec(subln_w))


def _hgrn2_kernel(q_ref, fl_ref, i_ref, g_ref, lbraw_ref, gw_ref, o_ref,
                  g_s, kf_s, qd_s, intra_s, upd_s, eg_s, st_s, *, nchunks, layer):
    c = REC_CHUNK
    nt = (((1,), (1,)), ((), ()))
    tn = (((0,), (0,)), ((), ()))
    raw = lbraw_ref[...]
    e = jnp.exp(raw - jnp.max(raw, axis=0, keepdims=True))
    lb = jnp.sum(e[0:layer + 1], axis=0, keepdims=True) / jnp.sum(e, axis=0, keepdims=True)
    gw = gw_ref[...]
    seq = nchunks * c
    grp = REC_GROUP
    gr = grp * c
    ngroups = nchunks // grp

    f = lb + (1.0 - lb) * jax.nn.sigmoid(fl_ref[...])
    kf_s[...] = 1.0 - f
    g = jnp.log(f)
    pos = lax.broadcasted_iota(jnp.int32, (seq, HEAD_W), 0) % c
    shift = 1
    while shift < c:
        g = g + jnp.where(pos >= shift, pltpu.roll(g, shift, 0), 0.0)
        shift *= 2
    g_s[...] = g
    factorable = jnp.min(g) >= -REC_SAFE_SPAN

    def chunk_last(gcum):
        n = gcum.shape[0] // c
        return jnp.concatenate(
            [jnp.broadcast_to(gcum[(j + 1) * c - 1:(j + 1) * c, :], (c, HEAD_W)) for j in range(n)],
            axis=0)

    for gi in range(ngroups):
        rows = slice(gi * gr, (gi + 1) * gr)
        q = q_ref[rows, :].astype(F32)
        gcum = g_s[rows, :]
        qd_s[rows, :] = (q * jax.nn.sigmoid(q) * jnp.exp(gcum)).astype(BF16)
        glast = chunk_last(gcum)
        kd = (kf_s[rows, :] * jnp.exp(glast - gcum)).astype(BF16)
        v = i_ref[rows, :]
        for j in range(grp):
            ci = gi * grp + j
            sub = slice(j * c, (j + 1) * c)
            upd_s[ci] = lax.dot_general(v[sub, :], kd[sub, :], tn, preferred_element_type=F32)
            eg_s[ci] = jnp.exp(glast[j * c:j * c + SUBLANES, :])

    @pl.when(factorable)
    def _():
        row = lax.broadcasted_iota(jnp.int32, (gr, gr), 0)
        col = lax.broadcasted_iota(jnp.int32, (gr, gr), 1)
        keep = (row >= col) & ((row // c) == (col // c))
        for gi in range(ngroups):
            rows = slice(gi * gr, (gi + 1) * gr)
            ku = (kf_s[rows, :] * jnp.exp(-g_s[rows, :])).astype(BF16)
            a = lax.dot_general(qd_s[rows, :], ku, nt, preferred_element_type=F32)
            a = jnp.where(keep, a, 0.0).astype(BF16)
            intra_s[rows, :] = jnp.dot(a, i_ref[rows, :], preferred_element_type=F32)

    @pl.when(jnp.logical_not(factorable))
    def _():
        row = lax.broadcasted_iota(jnp.int32, (c, c), 0)
        col = lax.broadcasted_iota(jnp.int32, (c, c), 1)

        def chunk(ci, _):
            rows = pl.ds(pl.multiple_of(ci * c, c), c)
            q = q_ref[rows, :].astype(F32)
            qf = q * jax.nn.sigmoid(q)
            gcum = g_s[rows, :]

            def column(s, a):
                src = pl.ds(ci * c + s, 1)
                term = qf * kf_s[src, :] * jnp.exp(jnp.minimum(gcum - g_s[src, :], 0.0))
                return jnp.where(col == s, jnp.sum(term, axis=1, keepdims=True), a)

            a = lax.fori_loop(0, c, column, jnp.zeros((c, c), F32))
            a = jnp.where(row >= col, a, 0.0).astype(BF16)
            intra_s[rows, :] = jnp.dot(a, i_ref[rows, :], preferred_element_type=F32)
            return 0

        lax.fori_loop(0, nchunks, chunk, 0)

    st = jnp.zeros((HEAD_W, HEAD_W), F32)
    for ci in range(nchunks):
        st_s[ci] = st.astype(BF16)
        st = st * jnp.broadcast_to(eg_s[ci][0:1, :], st.shape) + upd_s[ci]

    for gi in range(ngroups):
        rows = slice(gi * gr, (gi + 1) * gr)
        inter = jnp.concatenate(
            [lax.dot_general(qd_s[(gi * grp + j) * c:(gi * grp + j + 1) * c, :], st_s[gi * grp + j], nt,
                             preferred_element_type=F32) for j in range(grp)], axis=0)
        o = _rms(inter + intra_s[rows, :], gw)
        gate = g_ref[rows, :].astype(F32)
        o_ref[rows, :] = (o * (gate * jax.nn.sigmoid(gate))).astype(BF16)


def _hgrn2(proj, fl, lb_raw, gnorm_w, batch, seq, *, layer=0):
    t = proj.shape[0]
    heads = fl.shape[1] // HEAD_W
    gpt = PROJ_TN // HEAD_W
    depth1 = lb_raw.shape[0]
    assert seq % (REC_CHUNK * REC_GROUP) == 0
    nchunks = seq // REC_CHUNK
    blk = lambda col: pl.BlockSpec((seq, HEAD_W), lambda b, h: (b, col * gpt + h))
    return pl.pallas_call(
        functools.partial(_hgrn2_kernel, nchunks=nchunks, layer=layer),
        grid=(batch, heads),
        in_specs=[
            blk(COL_QR),
            pl.BlockSpec((seq, HEAD_W), lambda b, h: (b, h)),
            blk(COL_IR),
            blk(COL_GR),
            pl.BlockSpec((depth1, HEAD_W), lambda b, h: (0, h)),
            pl.BlockSpec((1, HEAD_W), lambda b, h: (0, 0)),
        ],
        out_specs=pl.BlockSpec((seq, HEAD_W), lambda b, h: (b, h)),
        out_shape=jax.ShapeDtypeStruct((t, heads * HEAD_W), BF16),
        scratch_shapes=[
            pltpu.VMEM((seq, HEAD_W), F32),
            pltpu.VMEM((seq, HEAD_W), F32),
            pltpu.VMEM((seq, HEAD_W), BF16),
            pltpu.VMEM((seq, HEAD_W), F32),
            pltpu.VMEM((nchunks, HEAD_W, HEAD_W), F32),
            pltpu.VMEM((nchunks, SUBLANES, HEAD_W), F32),
            pltpu.VMEM((nchunks, HEAD_W, HEAD_W), BF16),
        ],
        compiler_params=_params(("parallel", "parallel")),
        name="hgrn2",
    )(proj, fl, proj, proj, lb_raw.astype(F32), gnorm_w.reshape(1, -1).astype(F32))


def _merge_kernel(x_ref, ya_ref, yr_ref, ga0_ref, ga1_ref, gb0_ref, gb1_ref, wa_ref, wr_ref, wo_ref,
                  o_ref):
    pa = jnp.dot(ya_ref[...], wa_ref[...], preferred_element_type=F32)
    pr = jnp.dot(yr_ref[...], wr_ref[...], preferred_element_type=F32)
    ga = jnp.concatenate([ga0_ref[...], ga1_ref[...]], axis=1).astype(F32)
    gb = jnp.concatenate([gb0_ref[...], gb1_ref[...]], axis=1).astype(F32)
    merged = jax.nn.sigmoid(ga) * pa + jax.nn.sigmoid(gb) * pr
    o_ref[...] = x_ref[...] + jnp.dot(merged.astype(BF16), wo_ref[...], preferred_element_type=F32)


def _merge(x, ya, yr, proj, wa, wr, wo, *, tm=256):
    t, d = x.shape
    tm = min(tm, t)
    wa_k, wr_k = wa.shape[0], wr.shape[0]
    assert d == 2 * PROJ_TN
    const = lambda shape: pl.BlockSpec(shape, lambda i: (0, 0), pipeline_mode=pl.Buffered(1))
    gate = lambda col: pl.BlockSpec((tm, PROJ_TN), lambda i: (i, col))
    return pl.pallas_call(
        _merge_kernel,
        grid=(t // tm,),
        in_specs=[
            pl.BlockSpec((tm, d), lambda i: (i, 0)),
            pl.BlockSpec((tm, wa_k), lambda i: (i, 0)),
            pl.BlockSpec((tm, wr_k), lambda i: (i, 0)),
            gate(COL_GA), gate(COL_GA + 1), gate(COL_GB), gate(COL_GB + 1),
            const((wa_k, d)), const((wr_k, d)), const((d, d)),
        ],
        out_specs=pl.BlockSpec((tm, d), lambda i: (i, 0)),
        out_shape=jax.ShapeDtypeStruct((t, d), F32),
        compiler_params=_params(("parallel",)),
        name="merge",
    )(x, ya, yr, proj, proj, proj, proj, wa, wr, wo)


def kernel(x, ffn1_norm, ffn1_in, ffn1_out, mix_norm, w_in, lambda_q1, lambda_k1, lambda_q2,
           lambda_k2, attn_subln, rec_lb_raw, rec_gnorm, w_proj_attn, w_proj_rec, w_out,
           ffn2_norm, ffn2_in, ffn2_out, final_norm):
    batch, seq, d = x.shape
    depth = ffn1_in.shape[0]
    bf = lambda w: w.astype(BF16)
    xt = x.reshape(batch * seq, d)
    for l in range(depth):
        xt = _ffn(xt, ffn1_norm[l], bf(ffn1_in[l]), bf(ffn1_out[l]), final_norm, final_norm=False)
        proj, fl = _proj(xt, mix_norm[l], bf(w_in[l]), seq)
        ya = _attn(proj, lambda_q1[l], lambda_k1[l], lambda_q2[l], lambda_k2[l], attn_subln[l],
                   batch, seq, layer=l)
        yr = _hgrn2(proj, fl, rec_lb_raw, rec_gnorm[l], batch, seq, layer=l)
        xt = _merge(xt, ya, yr, proj, bf(w_proj_attn[l]), bf(w_proj_rec[l]), bf(w_out[l]))
        xt = _ffn(xt, ffn2_norm[l], bf(ffn2_in[l]), bf(ffn2_out[l]), final_norm,
                  final_norm=(l == depth - 1))
    return xt.reshape(batch, seq, d)
```

```python
import functools
import math

import jax
import jax.numpy as jnp
from jax import lax
from jax.experimental import pallas as pl
from jax.experimental.pallas import tpu as pltpu

F32 = jnp.float32
BF16 = jnp.bfloat16

EPS = 1e-6
ATTN_HEADS = 8
ATTN_HEAD_DIM = 64
ROPE_THETA = 500000.0
ROT_DIM = ATTN_HEAD_DIM // 4
HEAD_W = 128
REC_CHUNK = 64
REC_GROUP = 4
REC_SAFE_SPAN = 80.0
LANES = 128
SUBLANES = 8
PROJ_TN = 1024
COL_QA, COL_KA, COL_VA, COL_QR, COL_FR, COL_IR, COL_GR, COL_GA, COL_GB = 0, 1, 2, 3, 4, 5, 6, 7, 9
VMEM_LIMIT = 52 * 1024 * 1024


def _rms(x, w):
    return x * lax.rsqrt(jnp.mean(x * x, axis=-1, keepdims=True) + EPS) * w


def _params(sem):
    return pltpu.CompilerParams(dimension_semantics=sem, vmem_limit_bytes=VMEM_LIMIT)


def _row_tile_prefetch(x_hbm, xbuf, sem, consume):
    i, s = pl.program_id(0), pl.program_id(1)
    tm = xbuf.shape[0]

    def copy(tile):
        return pltpu.make_async_copy(x_hbm.at[pl.ds(pl.multiple_of(tile * tm, tm), tm), :], xbuf, sem)

    @pl.when((s == 0) & (i == 0))
    def _():
        copy(0).start()

    @pl.when(s == 0)
    def _():
        copy(i).wait()
        consume(xbuf[...])

    @pl.when((s == 1) & (i + 1 < pl.num_programs(0)))
    def _():
        copy(i + 1).start()


def _ffn_kernel(x_hbm, nw_ref, wg_ref, wu_ref, wo_ref, fw_ref, o_ref, h_ref, xbuf, sem, *, nf,
                final_norm):
    f = pl.program_id(1)

    def first_step(x):
        h_ref[...] = _rms(x, nw_ref[...]).astype(BF16)
        o_ref[...] = x

    _row_tile_prefetch(x_hbm, xbuf, sem, first_step)

    h = h_ref[...]
    g = jnp.dot(h, wg_ref[...], preferred_element_type=F32)
    u = jnp.dot(h, wu_ref[...], preferred_element_type=F32)
    a = (g * jax.nn.sigmoid(g) * u * 0.5).astype(BF16)
    o_ref[...] += jnp.dot(a, wo_ref[...], preferred_element_type=F32)

    if final_norm:
        @pl.when(f == nf - 1)
        def _():
            o_ref[...] = _rms(o_ref[...], fw_ref[...])


def _ffn(x, norm_w, w_in, w_out, final_w, *, final_norm, tm=1024, tf=512):
    t, d = x.shape
    ff = w_out.shape[0]
    tm, tf = min(tm, t), min(tf, ff)
    nf = ff // tf
    assert t % tm == 0 and ff % tf == 0 and nf >= 2
    return pl.pallas_call(
        functools.partial(_ffn_kernel, nf=nf, final_norm=final_norm),
        grid=(t // tm, nf),
        in_specs=[
            pl.BlockSpec(memory_space=pl.ANY),
            pl.BlockSpec((1, d), lambda i, f: (0, 0)),
            pl.BlockSpec((d, tf), lambda i, f: (0, f)),
            pl.BlockSpec((d, tf), lambda i, f: (0, f + nf)),
            pl.BlockSpec((tf, d), lambda i, f: (f, 0)),
            pl.BlockSpec((1, d), lambda i, f: (0, 0)),
        ],
        out_specs=pl.BlockSpec((tm, d), lambda i, f: (i, 0)),
        out_shape=jax.ShapeDtypeStruct((t, d), F32),
        scratch_shapes=[pltpu.VMEM((tm, d), BF16), pltpu.VMEM((tm, d), F32),
                        pltpu.SemaphoreType.DMA(())],
        compiler_params=_params(("arbitrary", "arbitrary")),
        name="ffn",
    )(x, norm_w.reshape(1, d), w_in, w_in, w_out, final_w.reshape(1, d))


def _proj_kernel(x_hbm, nw_ref, w_ref, cos_ref, sina_ref, sinb_ref, o_ref, fl_ref, h_ref, xbuf, sem):
    j = pl.program_id(1)

    def first_step(x):
        h_ref[...] = _rms(x, nw_ref[...]).astype(BF16)

    _row_tile_prefetch(x_hbm, xbuf, sem, first_step)

    r = jnp.dot(h_ref[...], w_ref[...], preferred_element_type=F32)
    o_ref[...] = r.astype(BF16)

    @pl.when(j <= COL_KA)
    def _():
        half = ROT_DIM // 2
        scale = jnp.where(j == COL_QA, ATTN_HEAD_DIM ** -0.5 * math.log2(math.e), 1.0).astype(F32)
        cos = cos_ref[...] * scale
        sina = sina_ref[...] * scale
        sinb = sinb_ref[...] * scale
        for c0 in range(0, r.shape[1], LANES):
            rc = r[:, c0:c0 + LANES]
            rot = rc * cos + pltpu.roll(rc, LANES - half, 1) * sina + pltpu.roll(rc, half, 1) * sinb
            o_ref[:, c0:c0 + LANES] = rot.astype(BF16)

    @pl.when(j == COL_FR)
    def _():
        fl_ref[...] = r


def _rope_tables(seq):
    half = ROT_DIM // 2
    inv_freq = ROPE_THETA ** (-jnp.arange(0, ROT_DIM, 2, dtype=F32) / ROT_DIM)
    ang = jnp.arange(seq).astype(F32)[:, None] * inv_freq[None, :]
    cos8, sin8 = jnp.cos(ang), jnp.sin(ang)
    pad = ATTN_HEAD_DIM - ROT_DIM
    ones = jnp.ones((seq, pad), F32)
    zeros = jnp.zeros((seq, pad), F32)
    z8 = jnp.zeros((seq, half), F32)
    cos = jnp.concatenate([cos8, cos8, ones], axis=1)
    sina = jnp.concatenate([-sin8, z8, zeros], axis=1)
    sinb = jnp.concatenate([z8, sin8, zeros], axis=1)
    rep = LANES // ATTN_HEAD_DIM
    return tuple(jnp.tile(a, (1, rep)) for a in (cos, sina, sinb))


def _proj(x, norm_w, w, seq, *, tm=1024):
    t, d = x.shape
    n = w.shape[1]
    tm = min(tm, seq)
    tn = PROJ_TN
    assert t % tm == 0 and seq % tm == 0 and n % tn == 0
    spt = seq // tm
    cos, sina, sinb = _rope_tables(seq)
    tab_spec = pl.BlockSpec((tm, LANES), lambda i, j: (i % spt, 0))
    return pl.pallas_call(
        _proj_kernel,
        grid=(t // tm, n // tn),
        in_specs=[
            pl.BlockSpec(memory_space=pl.ANY),
            pl.BlockSpec((1, d), lambda i, j: (0, 0)),
            pl.BlockSpec((d, tn), lambda i, j: (0, j)),
            tab_spec, tab_spec, tab_spec,
        ],
        out_specs=[
            pl.BlockSpec((tm, tn), lambda i, j: (i, j)),
            pl.BlockSpec((tm, tn), lambda i, j: (i, 0)),
        ],
        out_shape=[
            jax.ShapeDtypeStruct((t, n), BF16),
            jax.ShapeDtypeStruct((t, tn), F32),
        ],
        scratch_shapes=[pltpu.VMEM((tm, d), BF16), pltpu.VMEM((tm, d), F32),
                        pltpu.SemaphoreType.DMA(())],
        compiler_params=_params(("arbitrary", "arbitrary")),
        name="proj",
    )(x, norm_w.reshape(1, d), w, cos, sina, sinb)


def _attn_kernel(q_ref, k_ref, v_ref, lq1_ref, lk1_ref, lq2_ref, lk2_ref, sw_ref, o_ref,
                 vx_ref, *, tq, lambda_init):
    hd = ATTN_HEAD_DIM
    w = HEAD_W
    hq = tq // 2
    seq = q_ref.shape[0]
    nt = (((1,), (1,)), ((), ()))
    vx_ref[:, 0:w] = v_ref[...]
    vx_ref[:, w:2 * w] = jnp.ones(v_ref.shape, BF16)
    lam = (jnp.exp(jnp.sum(lq1_ref[...] * lk1_ref[...], axis=-1, keepdims=True))
           - jnp.exp(jnp.sum(lq2_ref[...] * lk2_ref[...], axis=-1, keepdims=True))
           + lambda_init)
    lane = lax.broadcasted_iota(jnp.int32, (hq, w), 1)
    row = lax.broadcasted_iota(jnp.int32, (hq, hq), 0)
    col = lax.broadcasted_iota(jnp.int32, (hq, hq), 1)
    tri = row >= col
    tri2 = jnp.concatenate([tri, tri], axis=0)
    tri2_all = jnp.concatenate([tri2, jnp.ones((2 * hq, hq), jnp.bool_)], axis=0)

    def update(m, acc, s, vx):
        m_new = jnp.maximum(m, jnp.max(s, axis=-1, keepdims=True))
        alpha = jnp.exp2(m - m_new)
        p = jnp.exp2(s - jnp.concatenate([m_new] * (s.shape[1] // w), axis=1))
        acc = (jnp.concatenate([alpha, alpha], axis=1) * acc
               + jnp.dot(p.astype(BF16), vx, preferred_element_type=F32))
        return m_new, acc

    for qi in range(seq // tq):
        parts = []
        for r0 in (qi * tq, qi * tq + hq):
            q = q_ref[r0:r0 + hq, :]
            zero = jnp.zeros_like(q)
            parts += [jnp.where(lane < hd, q, zero), jnp.where(lane >= hd, q, zero)]
        qs = jnp.concatenate(parts, axis=0)
        m = jnp.full((2 * tq, w), -jnp.inf, F32)
        acc = jnp.zeros((2 * tq, 2 * w), F32)
        for kb in range(qi):
            keys = slice(kb * tq, (kb + 1) * tq)
            s = lax.dot_general(qs, k_ref[keys, :], nt, preferred_element_type=F32)
            m, acc = update(m, acc, s, vx_ref[keys, :])
        keys = slice(qi * tq, qi * tq + hq)
        s = lax.dot_general(qs, k_ref[keys, :], nt, preferred_element_type=F32)
        m, acc = update(m, acc, jnp.where(tri2_all, s, -jnp.inf), vx_ref[keys, :])
        keys = slice(qi * tq + hq, (qi + 1) * tq)
        s = lax.dot_general(qs[tq:, :], k_ref[keys, :], nt, preferred_element_type=F32)
        m_hi, acc_hi = update(m[tq:, :], acc[tq:, :], jnp.where(tri2, s, -jnp.inf), vx_ref[keys, :])
        acc = jnp.concatenate([acc[:tq, :], acc_hi], axis=0)
        o = acc[:, 0:w] / acc[:, w:2 * w]
        o = jnp.concatenate([o[0:hq, :] - lam * o[hq:tq, :],
                             o[tq:tq + hq, :] - lam * o[tq + hq:2 * tq, :]], axis=0)
        o_ref[qi * tq:(qi + 1) * tq, :] = (_rms(o, sw_ref[...]) * (1.0 - lambda_init)).astype(BF16)


def _attn(proj, lq1, lk1, lq2, lk2, subln_w, batch, seq, *, layer=0, tq=512):
    t = proj.shape[0]
    tq = min(tq, seq)
    assert seq % tq == 0 and tq % HEAD_W == 0
    heads = ATTN_HEADS
    gpt = PROJ_TN // HEAD_W
    lambda_init = 0.8 - 0.6 * math.exp(-0.3 * layer)
    vec = lambda a: a.reshape(1, -1).astype(F32)
    small = lambda w: pl.BlockSpec((1, w), lambda b, h: (0, 0))
    blk = lambda col: pl.BlockSpec((seq, HEAD_W), lambda b, h: (b, col * gpt + h))
    return pl.pallas_call(
        functools.partial(_attn_kernel, tq=tq, lambda_init=lambda_init),
        grid=(batch, heads),
        in_specs=[
            blk(COL_QA), blk(COL_KA), blk(COL_VA),
            small(ATTN_HEAD_DIM), small(ATTN_HEAD_DIM), small(ATTN_HEAD_DIM), small(ATTN_HEAD_DIM),
            small(HEAD_W),
        ],
        out_specs=pl.BlockSpec((seq, HEAD_W), lambda b, h: (b, h)),
        out_shape=jax.ShapeDtypeStruct((t, heads * HEAD_W), BF16),
        scratch_shapes=[pltpu.VMEM((seq, 2 * HEAD_W), BF16)],
        compiler_params=_params(("parallel", "parallel")),
        name="attn",
    )(proj, proj, proj, vec(lq1), vec(lk1), vec(lq2), vec(lk2), vec(subln_w))


def _hgrn2_kernel(q_ref, fl_ref, i_ref, g_ref, lbraw_ref, gw_ref, o_ref,
                  g_s, kf_s, qd_s, intra_s, upd_s, eg_s, st_s, *, nchunks, layer):
    c = REC_CHUNK
    nt = (((1,), (1,)), ((), ()))
    tn = (((0,), (0,)), ((), ()))
    raw = lbraw_ref[...]
    e = jnp.exp(raw - jnp.max(raw, axis=0, keepdims=True))
    lb = jnp.sum(e[0:layer + 1], axis=0, keepdims=True) / jnp.sum(e, axis=0, keepdims=True)
    gw = gw_ref[...]
    seq = nchunks * c
    grp = REC_GROUP
    gr = grp * c
    ngroups = nchunks // grp

    f = lb + (1.0 - lb) * jax.nn.sigmoid(fl_ref[...])
    kf_s[...] = 1.0 - f
    g = jnp.log(f)
    pos = lax.broadcasted_iota(jnp.int32, (seq, HEAD_W), 0) % c
    shift = 1
    while shift < c:
        g = g + jnp.where(pos >= shift, pltpu.roll(g, shift, 0), 0.0)
        shift *= 2
    g_s[...] = g
    factorable = jnp.min(g) >= -REC_SAFE_SPAN

    def chunk_last(gcum):
        n = gcum.shape[0] // c
        return jnp.concatenate(
            [jnp.broadcast_to(gcum[(j + 1) * c - 1:(j + 1) * c, :], (c, HEAD_W)) for j in range(n)],
            axis=0)

    for gi in range(ngroups):
        rows = slice(gi * gr, (gi + 1) * gr)
        q = q_ref[rows, :].astype(F32)
        gcum = g_s[rows, :]
        qd_s[rows, :] = (q * jax.nn.sigmoid(q) * jnp.exp(gcum)).astype(BF16)
        glast = chunk_last(gcum)
        kd = (kf_s[rows, :] * jnp.exp(glast - gcum)).astype(BF16)
        v = i_ref[rows, :]
        for j in range(grp):
            ci = gi * grp + j
            sub = slice(j * c, (j + 1) * c)
            upd_s[ci] = lax.dot_general(v[sub, :], kd[sub, :], tn, preferred_element_type=F32)
            eg_s[ci] = jnp.exp(glast[j * c:j * c + SUBLANES, :])

    @pl.when(factorable)
    def _():
        row = lax.broadcasted_iota(jnp.int32, (gr, gr), 0)
        col = lax.broadcasted_iota(jnp.int32, (gr, gr), 1)
        keep = (row >= col) & ((row // c) == (col // c))
        for gi in range(ngroups):
            rows = slice(gi * gr, (gi + 1) * gr)
            ku = (kf_s[rows, :] * jnp.exp(-g_s[rows, :])).astype(BF16)
            a = lax.dot_general(qd_s[rows, :], ku, nt, preferred_element_type=F32)
            a = jnp.where(keep, a, 0.0).astype(BF16)
            intra_s[rows, :] = jnp.dot(a, i_ref[rows, :], preferred_element_type=F32)

    @pl.when(jnp.logical_not(factorable))
    def _():
        row = lax.broadcasted_iota(jnp.int32, (c, c), 0)
        col = lax.broadcasted_iota(jnp.int32, (c, c), 1)

        def chunk(ci, _):
            rows = pl.ds(pl.multiple_of(ci * c, c), c)
            q = q_ref[rows, :].astype(F32)
            qf = q * jax.nn.sigmoid(q)
            gcum = g_s[rows, :]

            def column(s, a):
                src = pl.ds(ci * c + s, 1)
                term = qf * kf_s[src, :] * jnp.exp(jnp.minimum(gcum - g_s[src, :], 0.0))
                return jnp.where(col == s, jnp.sum(term, axis=1, keepdims=True), a)

            a = lax.fori_loop(0, c, column, jnp.zeros((c, c), F32))
            a = jnp.where(row >= col, a, 0.0).astype(BF16)
            intra_s[rows, :] = jnp.dot(a, i_ref[rows, :], preferred_element_type=F32)
            return 0

        lax.fori_loop(0, nchunks, chunk, 0)

    st = jnp.zeros((HEAD_W, HEAD_W), F32)
    for ci in range(nchunks):
        st_s[ci] = st.astype(BF16)
        st = st * jnp.broadcast_to(eg_s[ci][0:1, :], st.shape) + upd_s[ci]

    for gi in range(ngroups):
        rows = slice(gi * gr, (gi + 1) * gr)
        inter = jnp.concatenate(
            [lax.dot_general(qd_s[(gi * grp + j) * c:(gi * grp + j + 1) * c, :], st_s[gi * grp + j], nt,
                             preferred_element_type=F32) for j in range(grp)], axis=0)
        o = _rms(inter + intra_s[rows, :], gw)
        gate = g_ref[rows, :].astype(F32)
        o_ref[rows, :] = (o * (gate * jax.nn.sigmoid(gate))).astype(BF16)


def _hgrn2(proj, fl, lb_raw, gnorm_w, batch, seq, *, layer=0):
    t = proj.shape[0]
    heads = fl.shape[1] // HEAD_W
    gpt = PROJ_TN // HEAD_W
    depth1 = lb_raw.shape[0]
    assert seq % (REC_CHUNK * REC_GROUP) == 0
    nchunks = seq // REC_CHUNK
    blk = lambda col: pl.BlockSpec((seq, HEAD_W), lambda b, h: (b, col * gpt + h))
    return pl.pallas_call(
        functools.partial(_hgrn2_kernel, nchunks=nchunks, layer=layer),
        grid=(batch, heads),
        in_specs=[
            blk(COL_QR),
            pl.BlockSpec((seq, HEAD_W), lambda b, h: (b, h)),
            blk(COL_IR),
            blk(COL_GR),
            pl.BlockSpec((depth1, HEAD_W), lambda b, h: (0, h)),
            pl.BlockSpec((1, HEAD_W), lambda b, h: (0, 0)),
        ],
        out_specs=pl.BlockSpec((seq, HEAD_W), lambda b, h: (b, h)),
        out_shape=jax.ShapeDtypeStruct((t, heads * HEAD_W), BF16),
        scratch_shapes=[
            pltpu.VMEM((seq, HEAD_W), F32),
            pltpu.VMEM((seq, HEAD_W), F32),
            pltpu.VMEM((seq, HEAD_W), BF16),
            pltpu.VMEM((seq, HEAD_W), F32),
            pltpu.VMEM((nchunks, HEAD_W, HEAD_W), F32),
            pltpu.VMEM((nchunks, SUBLANES, HEAD_W), F32),
            pltpu.VMEM((nchunks, HEAD_W, HEAD_W), BF16),
        ],
        compiler_params=_params(("parallel", "parallel")),
        name="hgrn2",
    )(proj, fl, proj, proj, lb_raw.astype(F32), gnorm_w.reshape(1, -1).astype(F32))


def _merge_kernel(x_ref, ya_ref, yr_ref, ga0_ref, ga1_ref, gb0_ref, gb1_ref, wa_ref, wr_ref, wo_ref,
                  o_ref):
    pa = jnp.dot(ya_ref[...], wa_ref[...], preferred_element_type=F32)
    pr = jnp.dot(yr_ref[...], wr_ref[...], preferred_element_type=F32)
    ga = jnp.concatenate([ga0_ref[...], ga1_ref[...]], axis=1).astype(F32)
    gb = jnp.concatenate([gb0_ref[...], gb1_ref[...]], axis=1).astype(F32)
    merged = jax.nn.sigmoid(ga) * pa + jax.nn.sigmoid(gb) * pr
    o_ref[...] = x_ref[...] + jnp.dot(merged.astype(BF16), wo_ref[...], preferred_element_type=F32)


def _merge(x, ya, yr, proj, wa, wr, wo, *, tm=256):
    t, d = x.shape
    tm = min(tm, t)
    wa_k, wr_k = wa.shape[0], wr.shape[0]
    assert d == 2 * PROJ_TN
    const = lambda shape: pl.BlockSpec(shape, lambda i: (0, 0), pipeline_mode=pl.Buffered(1))
    gate = lambda col: pl.BlockSpec((tm, PROJ_TN), lambda i: (i, col))
    return pl.pallas_call(
        _merge_kernel,
        grid=(t // tm,),
        in_specs=[
            pl.BlockSpec((tm, d), lambda i: (i, 0)),
            pl.BlockSpec((tm, wa_k), lambda i: (i, 0)),
            pl.BlockSpec((tm, wr_k), lambda i: (i, 0)),
            gate(COL_GA), gate(COL_GA + 1), gate(COL_GB), gate(COL_GB + 1),
            const((wa_k, d)), const((wr_k, d)), const((d, d)),
        ],
        out_specs=pl.BlockSpec((tm, d), lambda i: (i, 0)),
        out_shape=jax.ShapeDtypeStruct((t, d), F32),
        compiler_params=_params(("parallel",)),
        name="merge",
    )(x, ya, yr, proj, proj, proj, proj, wa, wr, wo)


def kernel(x, ffn1_norm, ffn1_in, ffn1_out, mix_norm, w_in, lambda_q1, lambda_k1, lambda_q2,
           lambda_k2, attn_subln, rec_lb_raw, rec_gnorm, w_proj_attn, w_proj_rec, w_out,
           ffn2_norm, ffn2_in, ffn2_out, final_norm):
    batch, seq, d = x.shape
    depth = ffn1_in.shape[0]
    bf = lambda w: w.astype(BF16)
    xt = x.reshape(batch * seq, d)
    for l in range(depth):
        xt = _ffn(xt, ffn1_norm[l], bf(ffn1_in[l]), bf(ffn1_out[l]), final_norm, final_norm=False)
        proj, fl = _proj(xt, mix_norm[l], bf(w_in[l]), seq)
        ya = _attn(proj, lambda_q1[l], lambda_k1[l], lambda_q2[l], lambda_k2[l], attn_subln[l],
                   batch, seq, layer=l)
        yr = _hgrn2(proj, fl, rec_lb_raw, rec_gnorm[l], batch, seq, layer=l)
        xt = _merge(xt, ya, yr, proj, bf(w_proj_attn[l]), bf(w_proj_rec[l]), bf(w_out[l]))
        xt = _ffn(xt, ffn2_norm[l], bf(ffn2_in[l]), bf(ffn2_out[l]), final_norm,
                  final_norm=(l == depth - 1))
    return xt.reshape(batch, seq, d)
```

```python
import functools
import math

import jax
import jax.numpy as jnp
from jax import lax
from jax.experimental import pallas as pl
from jax.experimental.pallas import tpu as pltpu

F32 = jnp.float32
BF16 = jnp.bfloat16

EPS = 1e-6
ATTN_HEADS = 8
ATTN_HEAD_DIM = 64
ROPE_THETA = 500000.0
ROT_DIM = ATTN_HEAD_DIM // 4
HEAD_W = 128
REC_CHUNK = 64
REC_GROUP = 4
REC_SAFE_SPAN = 80.0
LANES = 128
SUBLANES = 8
BF16_ROWS = 16
PROJ_TN = 1024
COL_QA, COL_KA, COL_VA, COL_QR, COL_FR, COL_IR, COL_GR, COL_GA, COL_GB = 0, 1, 2, 3, 4, 5, 6, 7, 9
VMEM_LIMIT = 52 * 1024 * 1024
CAST_BLOCK_BYTES = 1024 * 1024


def _rms(x, w):
    return x * lax.rsqrt(jnp.mean(x * x, axis=-1, keepdims=True) + EPS) * w


def _params(sem):
    return pltpu.CompilerParams(dimension_semantics=sem, vmem_limit_bytes=VMEM_LIMIT)


def _row_tile_prefetch(x_hbm, xbuf, sem, consume):
    i, s = pl.program_id(0), pl.program_id(1)
    tm = xbuf.shape[0]

    def copy(tile):
        return pltpu.make_async_copy(x_hbm.at[pl.ds(pl.multiple_of(tile * tm, tm), tm), :], xbuf, sem)

    @pl.when((s == 0) & (i == 0))
    def _():
        copy(0).start()

    @pl.when(s == 0)
    def _():
        copy(i).wait()
        consume(xbuf[...])

    @pl.when((s == 1) & (i + 1 < pl.num_programs(0)))
    def _():
        copy(i + 1).start()


def _cast_plan(shape, ni, ninner):
    r, c = shape
    small = lambda rows, cols: rows * cols * 4 <= CAST_BLOCK_BYTES
    for nsteps, step in ((ni * ninner, lambda i, s: i * ninner + s), (ni, lambda i, s: i)):
        if r % nsteps == 0 and (r // nsteps) % BF16_ROWS == 0 and small(r // nsteps, c):
            return (r // nsteps, c), (lambda i, s, step=step: (step(i, s), 0))
        ncol = c // LANES
        if c % LANES == 0 and nsteps % ncol == 0:
            split = nsteps // ncol
            if r % split == 0 and (r // split) % BF16_ROWS == 0 and small(r // split, LANES):
                return ((r // split, LANES),
                        (lambda i, s, step=step, split=split: (step(i, s) % split, step(i, s) // split)))
    return None


def _cast_jobs(weights, ni, ninner):
    plans = [_cast_plan(w.shape, ni, ninner) for w in weights]
    ride = [(w, p) for w, p in zip(weights, plans) if p is not None]
    in_specs = [pl.BlockSpec(p[0], p[1]) for _, p in ride]
    out_shapes = [jax.ShapeDtypeStruct(w.shape, BF16) for w, _ in ride]
    return plans, [w for w, _ in ride], in_specs, out_shapes


def _merge_cast_results(weights, plans, cast_outs):
    outs = iter(cast_outs)
    return [next(outs) if p is not None else w.astype(BF16) for w, p in zip(weights, plans)]


def _ffn_kernel(x_hbm, nw_ref, wg_ref, wu_ref, wo_ref, fw_ref, *rest, nf, ncast, final_norm):
    cast_in = rest[:ncast]
    o_ref = rest[ncast]
    cast_out = rest[ncast + 1:2 * ncast + 1]
    h_ref, xbuf, sem = rest[2 * ncast + 1:]
    f = pl.program_id(1)
    for src, dst in zip(cast_in, cast_out):
        dst[...] = src[...].astype(BF16)

    def first_step(x):
        h_ref[...] = _rms(x, nw_ref[...]).astype(BF16)
        o_ref[...] = x

    _row_tile_prefetch(x_hbm, xbuf, sem, first_step)

    h = h_ref[...]
    g = jnp.dot(h, wg_ref[...], preferred_element_type=F32)
    u = jnp.dot(h, wu_ref[...], preferred_element_type=F32)
    a = (g * jax.nn.sigmoid(g) * u * 0.5).astype(BF16)
    o_ref[...] += jnp.dot(a, wo_ref[...], preferred_element_type=F32)

    if final_norm:
        @pl.when(f == nf - 1)
        def _():
            o_ref[...] = _rms(o_ref[...], fw_ref[...])


def _ffn(x, norm_w, w_in, w_out, final_w, *, final_norm, cast=(), tm=1024, tf=512):
    t, d = x.shape
    ff = w_out.shape[0]
    tm, tf = min(tm, t), min(tf, ff)
    nf = ff // tf
    assert t % tm == 0 and ff % tf == 0 and nf >= 2
    plans, cast_w, cast_specs, cast_shapes = _cast_jobs(cast, t // tm, nf)
    outs = pl.pallas_call(
        functools.partial(_ffn_kernel, nf=nf, ncast=len(cast_w), final_norm=final_norm),
        grid=(t // tm, nf),
        in_specs=[
            pl.BlockSpec(memory_space=pl.ANY),
            pl.BlockSpec((1, d), lambda i, f: (0, 0)),
            pl.BlockSpec((d, tf), lambda i, f: (0, f)),
            pl.BlockSpec((d, tf), lambda i, f: (0, f + nf)),
            pl.BlockSpec((tf, d), lambda i, f: (f, 0)),
            pl.BlockSpec((1, d), lambda i, f: (0, 0)),
        ] + cast_specs,
        out_specs=[pl.BlockSpec((tm, d), lambda i, f: (i, 0))] + cast_specs,
        out_shape=[jax.ShapeDtypeStruct((t, d), F32)] + cast_shapes,
        scratch_shapes=[pltpu.VMEM((tm, d), BF16), pltpu.VMEM((tm, d), F32),
                        pltpu.SemaphoreType.DMA(())],
        compiler_params=_params(("arbitrary", "arbitrary")),
        name="ffn",
    )(x, norm_w.reshape(1, d), w_in, w_in, w_out, final_w.reshape(1, d), *cast_w)
    return outs[0], _merge_cast_results(cast, plans, outs[1:])


def _proj_kernel(x_hbm, nw_ref, w_ref, cos_ref, sina_ref, sinb_ref, *rest, ncast):
    cast_in = rest[:ncast]
    o_ref, fl_ref = rest[ncast:ncast + 2]
    cast_out = rest[ncast + 2:2 * ncast + 2]
    h_ref, xbuf, sem = rest[2 * ncast + 2:]
    j = pl.program_id(1)
    for src, dst in zip(cast_in, cast_out):
        dst[...] = src[...].astype(BF16)

    def first_step(x):
        h_ref[...] = _rms(x, nw_ref[...]).astype(BF16)

    _row_tile_prefetch(x_hbm, xbuf, sem, first_step)

    r = jnp.dot(h_ref[...], w_ref[...], preferred_element_type=F32)
    o_ref[...] = r.astype(BF16)

    @pl.when(j <= COL_KA)
    def _():
        half = ROT_DIM // 2
        scale = jnp.where(j == COL_QA, ATTN_HEAD_DIM ** -0.5 * math.log2(math.e), 1.0).astype(F32)
        cos = cos_ref[...] * scale
        sina = sina_ref[...] * scale
        sinb = sinb_ref[...] * scale
        for c0 in range(0, r.shape[1], LANES):
            rc = r[:, c0:c0 + LANES]
            rot = rc * cos + pltpu.roll(rc, LANES - half, 1) * sina + pltpu.roll(rc, half, 1) * sinb
            o_ref[:, c0:c0 + LANES] = rot.astype(BF16)

    @pl.when(j == COL_FR)
    def _():
        fl_ref[...] = r


def _rope_tables(seq):
    half = ROT_DIM // 2
    inv_freq = ROPE_THETA ** (-jnp.arange(0, ROT_DIM, 2, dtype=F32) / ROT_DIM)
    ang = jnp.arange(seq).astype(F32)[:, None] * inv_freq[None, :]
    cos8, sin8 = jnp.cos(ang), jnp.sin(ang)
    pad = ATTN_HEAD_DIM - ROT_DIM
    ones = jnp.ones((seq, pad), F32)
    zeros = jnp.zeros((seq, pad), F32)
    z8 = jnp.zeros((seq, half), F32)
    cos = jnp.concatenate([cos8, cos8, ones], axis=1)
    sina = jnp.concatenate([-sin8, z8, zeros], axis=1)
    sinb = jnp.concatenate([z8, sin8, zeros], axis=1)
    rep = LANES // ATTN_HEAD_DIM
    return tuple(jnp.tile(a, (1, rep)) for a in (cos, sina, sinb))


def _proj(x, norm_w, w, seq, *, cast=(), tm=1024):
    t, d = x.shape
    n = w.shape[1]
    tm = min(tm, seq)
    tn = PROJ_TN
    assert t % tm == 0 and seq % tm == 0 and n % tn == 0
    spt = seq // tm
    cos, sina, sinb = _rope_tables(seq)
    tab_spec = pl.BlockSpec((tm, LANES), lambda i, j: (i % spt, 0))
    plans, cast_w, cast_specs, cast_shapes = _cast_jobs(cast, t // tm, n // tn)
    outs = pl.pallas_call(
        functools.partial(_proj_kernel, ncast=len(cast_w)),
        grid=(t // tm, n // tn),
        in_specs=[
            pl.BlockSpec(memory_space=pl.ANY),
            pl.BlockSpec((1, d), lambda i, j: (0, 0)),
            pl.BlockSpec((d, tn), lambda i, j: (0, j)),
            tab_spec, tab_spec, tab_spec,
        ] + cast_specs,
        out_specs=[
            pl.BlockSpec((tm, tn), lambda i, j: (i, j)),
            pl.BlockSpec((tm, tn), lambda i, j: (i, 0)),
        ] + cast_specs,
        out_shape=[
            jax.ShapeDtypeStruct((t, n), BF16),
            jax.ShapeDtypeStruct((t, tn), F32),
        ] + cast_shapes,
        scratch_shapes=[pltpu.VMEM((tm, d), BF16), pltpu.VMEM((tm, d), F32),
                        pltpu.SemaphoreType.DMA(())],
        compiler_params=_params(("arbitrary", "arbitrary")),
        name="proj",
    )(x, norm_w.reshape(1, d), w, cos, sina, sinb, *cast_w)
    return outs[0], outs[1], _merge_cast_results(cast, plans, outs[2:])


def _attn_kernel(q_ref, k_ref, v_ref, lq1_ref, lk1_ref, lq2_ref, lk2_ref, sw_ref, o_ref,
                 vx_ref, *, tq, lambda_init):
    hd = ATTN_HEAD_DIM
    w = HEAD_W
    hq = tq // 2
    seq = q_ref.shape[0]
    nt = (((1,), (1,)), ((), ()))
    vx_ref[:, 0:w] = v_ref[...]
    vx_ref[:, w:2 * w] = jnp.ones(v_ref.shape, BF16)
    lam = (jnp.exp(jnp.sum(lq1_ref[...] * lk1_ref[...], axis=-1, keepdims=True))
           - jnp.exp(jnp.sum(lq2_ref[...] * lk2_ref[...], axis=-1, keepdims=True))
           + lambda_init)
    lane = lax.broadcasted_iota(jnp.int32, (hq, w), 1)
    row = lax.broadcasted_iota(jnp.int32, (hq, hq), 0)
    col = lax.broadcasted_iota(jnp.int32, (hq, hq), 1)
    tri = row >= col
    tri2 = jnp.concatenate([tri, tri], axis=0)
    tri2_all = jnp.concatenate([tri2, jnp.ones((2 * hq, hq), jnp.bool_)], axis=0)

    def update(m, acc, s, vx):
        m_new = jnp.maximum(m, jnp.max(s, axis=-1, keepdims=True))
        alpha = jnp.exp2(m - m_new)
        p = jnp.exp2(s - jnp.concatenate([m_new] * (s.shape[1] // w), axis=1))
        acc = (jnp.concatenate([alpha, alpha], axis=1) * acc
               + jnp.dot(p.astype(BF16), vx, preferred_element_type=F32))
        return m_new, acc

    for qi in range(seq // tq):
        parts = []
        for r0 in (qi * tq, qi * tq + hq):
            q = q_ref[r0:r0 + hq, :]
            zero = jnp.zeros_like(q)
            parts += [jnp.where(lane < hd, q, zero), jnp.where(lane >= hd, q, zero)]
        qs = jnp.concatenate(parts, axis=0)
        m = jnp.full((2 * tq, w), -jnp.inf, F32)
        acc = jnp.zeros((2 * tq, 2 * w), F32)
        for kb in range(qi):
            keys = slice(kb * tq, (kb + 1) * tq)
            s = lax.dot_general(qs, k_ref[keys, :], nt, preferred_element_type=F32)
            m, acc = update(m, acc, s, vx_ref[keys, :])
        keys = slice(qi * tq, qi * tq + hq)
        s = lax.dot_general(qs, k_ref[keys, :], nt, preferred_element_type=F32)
        m, acc = update(m, acc, jnp.where(tri2_all, s, -jnp.inf), vx_ref[keys, :])
        keys = slice(qi * tq + hq, (qi + 1) * tq)
        s = lax.dot_general(qs[tq:, :], k_ref[keys, :], nt, preferred_element_type=F32)
        m_hi, acc_hi = update(m[tq:, :], acc[tq:, :], jnp.where(tri2, s, -jnp.inf), vx_ref[keys, :])
        acc = jnp.concatenate([acc[:tq, :], acc_hi], axis=0)
        o = acc[:, 0:w] / acc[:, w:2 * w]
        o = jnp.concatenate([o[0:hq, :] - lam * o[hq:tq, :],
                             o[tq:tq + hq, :] - lam * o[tq + hq:2 * tq, :]], axis=0)
        o_ref[qi * tq:(qi + 1) * tq, :] = (_rms(o, sw_ref[...]) * (1.0 - lambda_init)).astype(BF16)


def _attn(proj, lq1, lk1, lq2, lk2, subln_w, batch, seq, *, layer=0, tq=512):
    t = proj.shape[0]
    tq = min(tq, seq)
    assert seq % tq == 0 and tq % HEAD_W == 0
    heads = ATTN_HEADS
    gpt = PROJ_TN // HEAD_W
    lambda_init = 0.8 - 0.6 * math.exp(-0.3 * layer)
    vec = lambda a: a.reshape(1, -1).astype(F32)
    small = lambda w: pl.BlockSpec((1, w), lambda b, h: (0, 0))
    blk = lambda col: pl.BlockSpec((seq, HEAD_W), lambda b, h: (b, col * gpt + h))
    return pl.pallas_call(
        functools.partial(_attn_kernel, tq=tq, lambda_init=lambda_init),
        grid=(batch, heads),
        in_specs=[
            blk(COL_QA), blk(COL_KA), blk(COL_VA),
            small(ATTN_HEAD_DIM), small(ATTN_HEAD_DIM), small(ATTN_HEAD_DIM), small(ATTN_HEAD_DIM),
            small(HEAD_W),
        ],
        out_specs=pl.BlockSpec((seq, HEAD_W), lambda b, h: (b, h)),
        out_shape=jax.ShapeDtypeStruct((t, heads * HEAD_W), BF16),
        scratch_shapes=[pltpu.VMEM((seq, 2 * HEAD_W), BF16)],
        compiler_params=_params(("parallel", "parallel")),
        name="attn",
    )(proj, proj, proj, vec(lq1), vec(lk1), vec(lq2), vec(lk2), vec(subln_w))


def _hgrn2_kernel(q_ref, fl_ref, i_ref, g_ref, lbraw_ref, gw_ref, o_ref,
                  g_s, kf_s, qd_s, intra_s, upd_s, eg_s, st_s, *, nchunks, layer):
    c = REC_CHUNK
    nt = (((1,), (1,)), ((), ()))
    tn = (((0,), (0,)), ((), ()))
    raw = lbraw_ref[...]
    e = jnp.exp(raw - jnp.max(raw, axis=0, keepdims=True))
    lb = jnp.sum(e[0:layer + 1], axis=0, keepdims=True) / jnp.sum(e, axis=0, keepdims=True)
    gw = gw_ref[...]
    seq = nchunks * c
    grp = REC_GROUP
    gr = grp * c
    ngroups = nchunks // grp

    f = lb + (1.0 - lb) * jax.nn.sigmoid(fl_ref[...])
    kf_s[...] = 1.0 - f
    g = jnp.log(f)
    pos = lax.broadcasted_iota(jnp.int32, (seq, HEAD_W), 0) % c
    shift = 1
    while shift < c:
        g = g + jnp.where(pos >= shift, pltpu.roll(g, shift, 0), 0.0)
        shift *= 2
    g_s[...] = g
    factorable = jnp.min(g) >= -REC_SAFE_SPAN

    def chunk_last(gcum):
        n = gcum.shape[0] // c
        return jnp.concatenate(
            [jnp.broadcast_to(gcum[(j + 1) * c - 1:(j + 1) * c, :], (c, HEAD_W)) for j in range(n)],
            axis=0)

    for gi in range(ngroups):
        rows = slice(gi * gr, (gi + 1) * gr)
        q = q_ref[rows, :].astype(F32)
        gcum = g_s[rows, :]
        qd_s[rows, :] = (q * jax.nn.sigmoid(q) * jnp.exp(gcum)).astype(BF16)
        glast = chunk_last(gcum)
        kd = (kf_s[rows, :] * jnp.exp(glast - gcum)).astype(BF16)
        v = i_ref[rows, :]
        for j in range(grp):
            ci = gi * grp + j
            sub = slice(j * c, (j + 1) * c)
            upd_s[ci] = lax.dot_general(v[sub, :], kd[sub, :], tn, preferred_element_type=F32)
            eg_s[ci] = jnp.exp(glast[j * c:j * c + SUBLANES, :])

    @pl.when(factorable)
    def _():
        row = lax.broadcasted_iota(jnp.int32, (gr, gr), 0)
        col = lax.broadcasted_iota(jnp.int32, (gr, gr), 1)
        keep = (row >= col) & ((row // c) == (col // c))
        for gi in range(ngroups):
            rows = slice(gi * gr, (gi + 1) * gr)
            ku = (kf_s[rows, :] * jnp.exp(-g_s[rows, :])).astype(BF16)
            a = lax.dot_general(qd_s[rows, :], ku, nt, preferred_element_type=F32)
            a = jnp.where(keep, a, 0.0).astype(BF16)
            intra_s[rows, :] = jnp.dot(a, i_ref[rows, :], preferred_element_type=F32)

    @pl.when(jnp.logical_not(factorable))
    def _():
        row = lax.broadcasted_iota(jnp.int32, (c, c), 0)
        col = lax.broadcasted_iota(jnp.int32, (c, c), 1)

        def chunk(ci, _):
            rows = pl.ds(pl.multiple_of(ci * c, c), c)
            q = q_ref[rows, :].astype(F32)
            qf = q * jax.nn.sigmoid(q)
            gcum = g_s[rows, :]

            def column(s, a):
                src = pl.ds(ci * c + s, 1)
                term = qf * kf_s[src, :] * jnp.exp(jnp.minimum(gcum - g_s[src, :], 0.0))
                return jnp.where(col == s, jnp.sum(term, axis=1, keepdims=True), a)

            a = lax.fori_loop(0, c, column, jnp.zeros((c, c), F32))
            a = jnp.where(row >= col, a, 0.0).astype(BF16)
            intra_s[rows, :] = jnp.dot(a, i_ref[rows, :], preferred_element_type=F32)
            return 0

        lax.fori_loop(0, nchunks, chunk, 0)

    st = jnp.zeros((HEAD_W, HEAD_W), F32)
    for ci in range(nchunks):
        st_s[ci] = st.astype(BF16)
        st = st * jnp.broadcast_to(eg_s[ci][0:1, :], st.shape) + upd_s[ci]

    for gi in range(ngroups):
        rows = slice(gi * gr, (gi + 1) * gr)
        inter = jnp.concatenate(
            [lax.dot_general(qd_s[(gi * grp + j) * c:(gi * grp + j + 1) * c, :], st_s[gi * grp + j], nt,
                             preferred_element_type=F32) for j in range(grp)], axis=0)
        o = _rms(inter + intra_s[rows, :], gw)
        gate = g_ref[rows, :].astype(F32)
        o_ref[rows, :] = (o * (gate * jax.nn.sigmoid(gate))).astype(BF16)


def _hgrn2(proj, fl, lb_raw, gnorm_w, batch, seq, *, layer=0):
    t = proj.shape[0]
    heads = fl.shape[1] // HEAD_W
    gpt = PROJ_TN // HEAD_W
    depth1 = lb_raw.shape[0]
    assert seq % (REC_CHUNK * REC_GROUP) == 0
    nchunks = seq // REC_CHUNK
    blk = lambda col: pl.BlockSpec((seq, HEAD_W), lambda b, h: (b, col * gpt + h))
    return pl.pallas_call(
        functools.partial(_hgrn2_kernel, nchunks=nchunks, layer=layer),
        grid=(batch, heads),
        in_specs=[
            blk(COL_QR),
            pl.BlockSpec((seq, HEAD_W), lambda b, h: (b, h)),
            blk(COL_IR),
            blk(COL_GR),
            pl.BlockSpec((depth1, HEAD_W), lambda b, h: (0, h)),
            pl.BlockSpec((1, HEAD_W), lambda b, h: (0, 0)),
        ],
        out_specs=pl.BlockSpec((seq, HEAD_W), lambda b, h: (b, h)),
        out_shape=jax.ShapeDtypeStruct((t, heads * HEAD_W), BF16),
        scratch_shapes=[
            pltpu.VMEM((seq, HEAD_W), F32),
            pltpu.VMEM((seq, HEAD_W), F32),
            pltpu.VMEM((seq, HEAD_W), BF16),
            pltpu.VMEM((seq, HEAD_W), F32),
            pltpu.VMEM((nchunks, HEAD_W, HEAD_W), F32),
            pltpu.VMEM((nchunks, SUBLANES, HEAD_W), F32),
            pltpu.VMEM((nchunks, HEAD_W, HEAD_W), BF16),
        ],
        compiler_params=_params(("parallel", "parallel")),
        name="hgrn2",
    )(proj, fl, proj, proj, lb_raw.astype(F32), gnorm_w.reshape(1, -1).astype(F32))


def _merge_kernel(x_ref, ya_ref, yr_ref, ga0_ref, ga1_ref, gb0_ref, gb1_ref, wa_ref, wr_ref, wo_ref,
                  o_ref):
    pa = jnp.dot(ya_ref[...], wa_ref[...], preferred_element_type=F32)
    pr = jnp.dot(yr_ref[...], wr_ref[...], preferred_element_type=F32)
    ga = jnp.concatenate([ga0_ref[...], ga1_ref[...]], axis=1).astype(F32)
    gb = jnp.concatenate([gb0_ref[...], gb1_ref[...]], axis=1).astype(F32)
    merged = jax.nn.sigmoid(ga) * pa + jax.nn.sigmoid(gb) * pr
    o_ref[...] = x_ref[...] + jnp.dot(merged.astype(BF16), wo_ref[...], preferred_element_type=F32)


def _merge(x, ya, yr, proj, wa, wr, wo, *, tm=256):
    t, d = x.shape
    tm = min(tm, t)
    wa_k, wr_k = wa.shape[0], wr.shape[0]
    assert d == 2 * PROJ_TN
    const = lambda shape: pl.BlockSpec(shape, lambda i: (0, 0), pipeline_mode=pl.Buffered(1))
    gate = lambda col: pl.BlockSpec((tm, PROJ_TN), lambda i: (i, col))
    return pl.pallas_call(
        _merge_kernel,
        grid=(t // tm,),
        in_specs=[
            pl.BlockSpec((tm, d), lambda i: (i, 0)),
            pl.BlockSpec((tm, wa_k), lambda i: (i, 0)),
            pl.BlockSpec((tm, wr_k), lambda i: (i, 0)),
            gate(COL_GA), gate(COL_GA + 1), gate(COL_GB), gate(COL_GB + 1),
            const((wa_k, d)), const((wr_k, d)), const((d, d)),
        ],
        out_specs=pl.BlockSpec((tm, d), lambda i: (i, 0)),
        out_shape=jax.ShapeDtypeStruct((t, d), F32),
        compiler_params=_params(("parallel",)),
        name="merge",
    )(x, ya, yr, proj, proj, proj, proj, wa, wr, wo)


def kernel(x, ffn1_norm, ffn1_in, ffn1_out, mix_norm, w_in, lambda_q1, lambda_k1, lambda_q2,
           lambda_k2, attn_subln, rec_lb_raw, rec_gnorm, w_proj_attn, w_proj_rec, w_out,
           ffn2_norm, ffn2_in, ffn2_out, final_norm):
    batch, seq, d = x.shape
    depth = ffn1_in.shape[0]
    bf = lambda w: w.astype(BF16)
    xt = x.reshape(batch * seq, d)
    for l in range(depth):
        xt, (w_in_b, f2_in_b, f2_out_b) = _ffn(
            xt, ffn1_norm[l], bf(ffn1_in[l]), bf(ffn1_out[l]), final_norm, final_norm=False,
            cast=(w_in[l], ffn2_in[l], ffn2_out[l]))
        proj, fl, (wa_b, wr_b, wo_b) = _proj(xt, mix_norm[l], w_in_b, seq,
                                             cast=(w_proj_attn[l], w_proj_rec[l], w_out[l]))
        ya = _attn(proj, lambda_q1[l], lambda_k1[l], lambda_q2[l], lambda_k2[l], attn_subln[l],
                   batch, seq, layer=l)
        yr = _hgrn2(proj, fl, rec_lb_raw, rec_gnorm[l], batch, seq, layer=l)
        xt = _merge(xt, ya, yr, proj, wa_b, wr_b, wo_b)
        xt, _ = _ffn(xt, ffn2_norm[l], f2_in_b, f2_out_b, final_norm, final_norm=(l == depth - 1))
    return xt.reshape(batch, seq, d)
```

```python
import functools
import math

import jax
import jax.numpy as jnp
from jax import lax
from jax.experimental import pallas as pl
from jax.experimental.pallas import tpu as pltpu

F32 = jnp.float32
BF16 = jnp.bfloat16

EPS = 1e-6
ATTN_HEADS = 8
ATTN_HEAD_DIM = 64
ROPE_THETA = 500000.0
ROT_DIM = ATTN_HEAD_DIM // 4
HEAD_W = 128
REC_CHUNK = 64
REC_GROUP = 4
REC_SAFE_SPAN = 80.0
LANES = 128
SUBLANES = 8
BF16_ROWS = 16
FFN_TM = 1024
PROJ_TN = 1024
COL_QA, COL_KA, COL_VA, COL_QR, COL_FR, COL_IR, COL_GR, COL_GA, COL_GB = 0, 1, 2, 3, 4, 5, 6, 7, 9
VMEM_LIMIT = 52 * 1024 * 1024
CAST_BLOCK_BYTES = 1024 * 1024


def _rms(x, w):
    return x * lax.rsqrt(jnp.mean(x * x, axis=-1, keepdims=True) + EPS) * w


def _params(sem):
    return pltpu.CompilerParams(dimension_semantics=sem, vmem_limit_bytes=VMEM_LIMIT)


def _row_tile_prefetch(x_hbm, xbuf, sem, consume, first=0):
    i, s = pl.program_id(0), pl.program_id(1)
    tm = xbuf.shape[0]

    def copy(tile):
        return pltpu.make_async_copy(x_hbm.at[pl.ds(pl.multiple_of(tile * tm, tm), tm), :], xbuf, sem)

    if first == 0:
        @pl.when((s == 0) & (i == 0))
        def _():
            copy(0).start()

    @pl.when((s == 0) & (i >= first))
    def _():
        copy(i).wait()
        consume(xbuf[...])

    @pl.when((s == 1) & (i + 1 >= first) & (i + 1 < pl.num_programs(0)))
    def _():
        copy(i + 1).start()


def _cast_plan(shape, ni, ninner):
    r, c = shape
    small = lambda rows, cols: rows * cols * 4 <= CAST_BLOCK_BYTES
    for nsteps, step in ((ni * ninner, lambda i, s: i * ninner + s), (ni, lambda i, s: i)):
        if r % nsteps == 0 and (r // nsteps) % BF16_ROWS == 0 and small(r // nsteps, c):
            return (r // nsteps, c), (lambda i, s, step=step: (step(i, s), 0))
        ncol = c // LANES
        if c % LANES == 0 and nsteps % ncol == 0:
            split = nsteps // ncol
            if r % split == 0 and (r // split) % BF16_ROWS == 0 and small(r // split, LANES):
                return ((r // split, LANES),
                        (lambda i, s, step=step, split=split: (step(i, s) % split, step(i, s) // split)))
    return None


def _cast_jobs(weights, ni, ninner):
    plans = [_cast_plan(w.shape, ni, ninner) for w in weights]
    ride = [(w, p) for w, p in zip(weights, plans) if p is not None]
    in_specs = [pl.BlockSpec(p[0], p[1]) for _, p in ride]
    out_shapes = [jax.ShapeDtypeStruct(w.shape, BF16) for w, _ in ride]
    return plans, [w for w, _ in ride], in_specs, out_shapes


def _merge_cast_results(weights, plans, cast_outs):
    outs = iter(cast_outs)
    return [next(outs) if p is not None else w.astype(BF16) for w, p in zip(weights, plans)]


def _swiglu_half(h, wg, wu, wo):
    g = jnp.dot(h, wg, preferred_element_type=F32)
    u = jnp.dot(h, wu, preferred_element_type=F32)
    a = (g * jax.nn.sigmoid(g) * u * 0.5).astype(BF16)
    return jnp.dot(a, wo, preferred_element_type=F32)


def _ffn_head_kernel(x_ref, nw_ref, wg_ref, wu_ref, wo_ref, o_ref, wg_b, wu_b, wo_b, h_ref):
    @pl.when(pl.program_id(0) == 0)
    def _():
        x = x_ref[...]
        h_ref[...] = _rms(x, nw_ref[...]).astype(BF16)
        o_ref[...] = x

    wg_b[...] = wg_ref[...].astype(BF16)
    wu_b[...] = wu_ref[...].astype(BF16)
    wo_b[...] = wo_ref[...].astype(BF16)
    o_ref[...] += _swiglu_half(h_ref[...], wg_b[...], wu_b[...], wo_b[...])


def _ffn_head(x, norm_w, w_in, w_out, *, tm, tf=256):
    t, d = x.shape
    ff = w_out.shape[0]
    nf = ff // tf
    assert ff % tf == 0 and t % tm == 0
    return pl.pallas_call(
        _ffn_head_kernel,
        grid=(nf,),
        in_specs=[
            pl.BlockSpec((tm, d), lambda f: (0, 0), pipeline_mode=pl.Buffered(1)),
            pl.BlockSpec((1, d), lambda f: (0, 0)),
            pl.BlockSpec((d, tf), lambda f: (0, f)),
            pl.BlockSpec((d, tf), lambda f: (0, f + nf)),
            pl.BlockSpec((tf, d), lambda f: (f, 0)),
        ],
        out_specs=[
            pl.BlockSpec((tm, d), lambda f: (0, 0)),
            pl.BlockSpec((d, tf), lambda f: (0, f)),
            pl.BlockSpec((d, tf), lambda f: (0, f)),
            pl.BlockSpec((tf, d), lambda f: (f, 0)),
        ],
        out_shape=[
            jax.ShapeDtypeStruct((tm, d), F32),
            jax.ShapeDtypeStruct((d, ff), BF16),
            jax.ShapeDtypeStruct((d, ff), BF16),
            jax.ShapeDtypeStruct((ff, d), BF16),
        ],
        scratch_shapes=[pltpu.VMEM((tm, d), BF16)],
        compiler_params=_params(("arbitrary",)),
        name="ffn_head",
    )(x, norm_w.reshape(1, d), w_in, w_in, w_out)


def _ffn_kernel(x_hbm, nw_ref, wg_ref, wu_ref, wo_ref, fw_ref, *rest, nf, ncast, has_head,
                final_norm):
    head_hbm = rest[0] if has_head else None
    rest = rest[has_head:]
    cast_in = rest[:ncast]
    o_ref = rest[ncast]
    cast_out = rest[ncast + 1:2 * ncast + 1]
    h_ref, xbuf, sem = rest[2 * ncast + 1:]
    i, f = pl.program_id(0), pl.program_id(1)
    for src, dst in zip(cast_in, cast_out):
        dst[...] = src[...].astype(BF16)

    def first_step(x):
        h_ref[...] = _rms(x, nw_ref[...]).astype(BF16)
        o_ref[...] = x

    _row_tile_prefetch(x_hbm, xbuf, sem, first_step, first=int(has_head))

    def accumulate():
        o_ref[...] += _swiglu_half(h_ref[...], wg_ref[...], wu_ref[...], wo_ref[...])

    if has_head:
        @pl.when((i == 0) & (f == 0))
        def _():
            copy = pltpu.make_async_copy(head_hbm, o_ref, sem)
            copy.start()
            copy.wait()

        pl.when(i > 0)(accumulate)
    else:
        accumulate()

    if final_norm:
        @pl.when(f == nf - 1)
        def _():
            o_ref[...] = _rms(o_ref[...], fw_ref[...])


def _ffn(x, norm_w, w_gate, w_up, w_out, final_w, *, final_norm, head=None, cast=(), tm=1024, tf=512):
    t, d = x.shape
    ff = w_out.shape[0]
    tm, tf = min(tm, t), min(tf, ff)
    nf = ff // tf
    assert t % tm == 0 and ff % tf == 0 and nf >= 2
    assert head is None or (head.shape == (tm, d) and t // tm >= 2 and not final_norm)
    up0 = nf if w_up.shape[1] == 2 * ff else 0
    hold = (lambda i, f: f) if head is None else (lambda i, f: jnp.where(i == 0, 0, f))
    plans, cast_w, cast_specs, cast_shapes = _cast_jobs(cast, t // tm, nf)
    outs = pl.pallas_call(
        functools.partial(_ffn_kernel, nf=nf, ncast=len(cast_w), has_head=head is not None,
                          final_norm=final_norm),
        grid=(t // tm, nf),
        in_specs=[
            pl.BlockSpec(memory_space=pl.ANY),
            pl.BlockSpec((1, d), lambda i, f: (0, 0)),
            pl.BlockSpec((d, tf), lambda i, f: (0, hold(i, f))),
            pl.BlockSpec((d, tf), lambda i, f: (0, hold(i, f) + up0)),
            pl.BlockSpec((tf, d), lambda i, f: (hold(i, f), 0)),
            pl.BlockSpec((1, d), lambda i, f: (0, 0)),
        ] + ([pl.BlockSpec(memory_space=pl.ANY)] if head is not None else []) + cast_specs,
        out_specs=[pl.BlockSpec((tm, d), lambda i, f: (i, 0))] + cast_specs,
        out_shape=[jax.ShapeDtypeStruct((t, d), F32)] + cast_shapes,
        scratch_shapes=[pltpu.VMEM((tm, d), BF16), pltpu.VMEM((tm, d), F32),
                        pltpu.SemaphoreType.DMA(())],
        compiler_params=_params(("arbitrary", "arbitrary")),
        name="ffn",
    )(x, norm_w.reshape(1, d), w_gate, w_up, w_out, final_w.reshape(1, d),
      *([head] if head is not None else []), *cast_w)
    return outs[0], _merge_cast_results(cast, plans, outs[1:])


def _proj_kernel(x_hbm, nw_ref, w_ref, cos_ref, sina_ref, sinb_ref, *rest, ncast):
    cast_in = rest[:ncast]
    o_ref, fl_ref = rest[ncast:ncast + 2]
    cast_out = rest[ncast + 2:2 * ncast + 2]
    h_ref, xbuf, sem = rest[2 * ncast + 2:]
    j = pl.program_id(1)
    for src, dst in zip(cast_in, cast_out):
        dst[...] = src[...].astype(BF16)

    def first_step(x):
        h_ref[...] = _rms(x, nw_ref[...]).astype(BF16)

    _row_tile_prefetch(x_hbm, xbuf, sem, first_step)

    r = jnp.dot(h_ref[...], w_ref[...], preferred_element_type=F32)
    o_ref[...] = r.astype(BF16)

    @pl.when(j <= COL_KA)
    def _():
        half = ROT_DIM // 2
        scale = jnp.where(j == COL_QA, ATTN_HEAD_DIM ** -0.5 * math.log2(math.e), 1.0).astype(F32)
        cos = cos_ref[...] * scale
        sina = sina_ref[...] * scale
        sinb = sinb_ref[...] * scale
        for c0 in range(0, r.shape[1], LANES):
            rc = r[:, c0:c0 + LANES]
            rot = rc * cos + pltpu.roll(rc, LANES - half, 1) * sina + pltpu.roll(rc, half, 1) * sinb
            o_ref[:, c0:c0 + LANES] = rot.astype(BF16)

    @pl.when(j == COL_FR)
    def _():
        fl_ref[...] = r


def _rope_tables(seq):
    half = ROT_DIM // 2
    inv_freq = ROPE_THETA ** (-jnp.arange(0, ROT_DIM, 2, dtype=F32) / ROT_DIM)
    ang = jnp.arange(seq).astype(F32)[:, None] * inv_freq[None, :]
    cos8, sin8 = jnp.cos(ang), jnp.sin(ang)
    pad = ATTN_HEAD_DIM - ROT_DIM
    ones = jnp.ones((seq, pad), F32)
    zeros = jnp.zeros((seq, pad), F32)
    z8 = jnp.zeros((seq, half), F32)
    cos = jnp.concatenate([cos8, cos8, ones], axis=1)
    sina = jnp.concatenate([-sin8, z8, zeros], axis=1)
    sinb = jnp.concatenate([z8, sin8, zeros], axis=1)
    rep = LANES // ATTN_HEAD_DIM
    return tuple(jnp.tile(a, (1, rep)) for a in (cos, sina, sinb))


def _proj(x, norm_w, w, seq, *, cast=(), tm=1024):
    t, d = x.shape
    n = w.shape[1]
    tm = min(tm, seq)
    tn = PROJ_TN
    assert t % tm == 0 and seq % tm == 0 and n % tn == 0
    spt = seq // tm
    cos, sina, sinb = _rope_tables(seq)
    tab_spec = pl.BlockSpec((tm, LANES), lambda i, j: (i % spt, 0))
    plans, cast_w, cast_specs, cast_shapes = _cast_jobs(cast, t // tm, n // tn)
    outs = pl.pallas_call(
        functools.partial(_proj_kernel, ncast=len(cast_w)),
        grid=(t // tm, n // tn),
        in_specs=[
            pl.BlockSpec(memory_space=pl.ANY),
            pl.BlockSpec((1, d), lambda i, j: (0, 0)),
            pl.BlockSpec((d, tn), lambda i, j: (0, j)),
            tab_spec, tab_spec, tab_spec,
        ] + cast_specs,
        out_specs=[
            pl.BlockSpec((tm, tn), lambda i, j: (i, j)),
            pl.BlockSpec((tm, tn), lambda i, j: (i, 0)),
        ] + cast_specs,
        out_shape=[
            jax.ShapeDtypeStruct((t, n), BF16),
            jax.ShapeDtypeStruct((t, tn), F32),
        ] + cast_shapes,
        scratch_shapes=[pltpu.VMEM((tm, d), BF16), pltpu.VMEM((tm, d), F32),
                        pltpu.SemaphoreType.DMA(())],
        compiler_params=_params(("arbitrary", "arbitrary")),
        name="proj",
    )(x, norm_w.reshape(1, d), w, cos, sina, sinb, *cast_w)
    return outs[0], outs[1], _merge_cast_results(cast, plans, outs[2:])


def _attn_kernel(q_ref, k_ref, v_ref, lq1_ref, lk1_ref, lq2_ref, lk2_ref, sw_ref, o_ref,
                 vx_ref, *, tq, lambda_init):
    hd = ATTN_HEAD_DIM
    w = HEAD_W
    hq = tq // 2
    seq = q_ref.shape[0]
    nt = (((1,), (1,)), ((), ()))
    vx_ref[:, 0:w] = v_ref[...]
    vx_ref[:, w:2 * w] = jnp.ones(v_ref.shape, BF16)
    lam = (jnp.exp(jnp.sum(lq1_ref[...] * lk1_ref[...], axis=-1, keepdims=True))
           - jnp.exp(jnp.sum(lq2_ref[...] * lk2_ref[...], axis=-1, keepdims=True))
           + lambda_init)
    lane = lax.broadcasted_iota(jnp.int32, (hq, w), 1)
    row = lax.broadcasted_iota(jnp.int32, (hq, hq), 0)
    col = lax.broadcasted_iota(jnp.int32, (hq, hq), 1)
    tri = row >= col
    tri2 = jnp.concatenate([tri, tri], axis=0)
    tri2_all = jnp.concatenate([tri2, jnp.ones((2 * hq, hq), jnp.bool_)], axis=0)

    def update(m, acc, s, vx):
        m_new = jnp.maximum(m, jnp.max(s, axis=-1, keepdims=True))
        alpha = jnp.exp2(m - m_new)
        p = jnp.exp2(s - jnp.concatenate([m_new] * (s.shape[1] // w), axis=1))
        acc = (jnp.concatenate([alpha, alpha], axis=1) * acc
               + jnp.dot(p.astype(BF16), vx, preferred_element_type=F32))
        return m_new, acc

    for qi in range(seq // tq):
        parts = []
        for r0 in (qi * tq, qi * tq + hq):
            q = q_ref[r0:r0 + hq, :]
            zero = jnp.zeros_like(q)
            parts += [jnp.where(lane < hd, q, zero), jnp.where(lane >= hd, q, zero)]
        qs = jnp.concatenate(parts, axis=0)
        m = jnp.full((2 * tq, w), -jnp.inf, F32)
        acc = jnp.zeros((2 * tq, 2 * w), F32)
        for kb in range(qi):
            keys = slice(kb * tq, (kb + 1) * tq)
            s = lax.dot_general(qs, k_ref[keys, :], nt, preferred_element_type=F32)
            m, acc = update(m, acc, s, vx_ref[keys, :])
        keys = slice(qi * tq, qi * tq + hq)
        s = lax.dot_general(qs, k_ref[keys, :], nt, preferred_element_type=F32)
        m, acc = update(m, acc, jnp.where(tri2_all, s, -jnp.inf), vx_ref[keys, :])
        keys = slice(qi * tq + hq, (qi + 1) * tq)
        s = lax.dot_general(qs[tq:, :], k_ref[keys, :], nt, preferred_element_type=F32)
        m_hi, acc_hi = update(m[tq:, :], acc[tq:, :], jnp.where(tri2, s, -jnp.inf), vx_ref[keys, :])
        acc = jnp.concatenate([acc[:tq, :], acc_hi], axis=0)
        o = acc[:, 0:w] / acc[:, w:2 * w]
        o = jnp.concatenate([o[0:hq, :] - lam * o[hq:tq, :],
                             o[tq:tq + hq, :] - lam * o[tq + hq:2 * tq, :]], axis=0)
        o_ref[qi * tq:(qi + 1) * tq, :] = (_rms(o, sw_ref[...]) * (1.0 - lambda_init)).astype(BF16)


def _attn(proj, lq1, lk1, lq2, lk2, subln_w, batch, seq, *, layer=0, tq=512):
    t = proj.shape[0]
    tq = min(tq, seq)
    assert seq % tq == 0 and tq % HEAD_W == 0
    heads = ATTN_HEADS
    gpt = PROJ_TN // HEAD_W
    lambda_init = 0.8 - 0.6 * math.exp(-0.3 * layer)
    vec = lambda a: a.reshape(1, -1).astype(F32)
    small = lambda w: pl.BlockSpec((1, w), lambda b, h: (0, 0))
    blk = lambda col: pl.BlockSpec((seq, HEAD_W), lambda b, h: (b, col * gpt + h))
    return pl.pallas_call(
        functools.partial(_attn_kernel, tq=tq, lambda_init=lambda_init),
        grid=(batch, heads),
        in_specs=[
            blk(COL_QA), blk(COL_KA), blk(COL_VA),
            small(ATTN_HEAD_DIM), small(ATTN_HEAD_DIM), small(ATTN_HEAD_DIM), small(ATTN_HEAD_DIM),
            small(HEAD_W),
        ],
        out_specs=pl.BlockSpec((seq, HEAD_W), lambda b, h: (b, h)),
        out_shape=jax.ShapeDtypeStruct((t, heads * HEAD_W), BF16),
        scratch_shapes=[pltpu.VMEM((seq, 2 * HEAD_W), BF16)],
        compiler_params=_params(("parallel", "parallel")),
        name="attn",
    )(proj, proj, proj, vec(lq1), vec(lk1), vec(lq2), vec(lk2), vec(subln_w))


def _hgrn2_kernel(q_ref, fl_ref, i_ref, g_ref, lbraw_ref, gw_ref, o_ref,
                  g_s, kf_s, qd_s, intra_s, upd_s, eg_s, st_s, *, nchunks, layer):
    c = REC_CHUNK
    nt = (((1,), (1,)), ((), ()))
    tn = (((0,), (0,)), ((), ()))
    raw = lbraw_ref[...]
    e = jnp.exp(raw - jnp.max(raw, axis=0, keepdims=True))
    lb = jnp.sum(e[0:layer + 1], axis=0, keepdims=True) / jnp.sum(e, axis=0, keepdims=True)
    gw = gw_ref[...]
    seq = nchunks * c
    grp = REC_GROUP
    gr = grp * c
    ngroups = nchunks // grp

    f = lb + (1.0 - lb) * jax.nn.sigmoid(fl_ref[...])
    kf_s[...] = 1.0 - f
    g = jnp.log(f)
    pos = lax.broadcasted_iota(jnp.int32, (seq, HEAD_W), 0) % c
    shift = 1
    while shift < c:
        g = g + jnp.where(pos >= shift, pltpu.roll(g, shift, 0), 0.0)
        shift *= 2
    g_s[...] = g
    factorable = jnp.min(g) >= -REC_SAFE_SPAN

    def chunk_last(gcum):
        n = gcum.shape[0] // c
        return jnp.concatenate(
            [jnp.broadcast_to(gcum[(j + 1) * c - 1:(j + 1) * c, :], (c, HEAD_W)) for j in range(n)],
            axis=0)

    for gi in range(ngroups):
        rows = slice(gi * gr, (gi + 1) * gr)
        q = q_ref[rows, :].astype(F32)
        gcum = g_s[rows, :]
        qd_s[rows, :] = (q * jax.nn.sigmoid(q) * jnp.exp(gcum)).astype(BF16)
        glast = chunk_last(gcum)
        kd = (kf_s[rows, :] * jnp.exp(glast - gcum)).astype(BF16)
        v = i_ref[rows, :]
        for j in range(grp):
            ci = gi * grp + j
            sub = slice(j * c, (j + 1) * c)
            upd_s[ci] = lax.dot_general(v[sub, :], kd[sub, :], tn, preferred_element_type=F32)
            eg_s[ci] = jnp.exp(glast[j * c:j * c + SUBLANES, :])

    @pl.when(factorable)
    def _():
        row = lax.broadcasted_iota(jnp.int32, (gr, gr), 0)
        col = lax.broadcasted_iota(jnp.int32, (gr, gr), 1)
        keep = (row >= col) & ((row // c) == (col // c))
        for gi in range(ngroups):
            rows = slice(gi * gr, (gi + 1) * gr)
            ku = (kf_s[rows, :] * jnp.exp(-g_s[rows, :])).astype(BF16)
            a = lax.dot_general(qd_s[rows, :], ku, nt, preferred_element_type=F32)
            a = jnp.where(keep, a, 0.0).astype(BF16)
            intra_s[rows, :] = jnp.dot(a, i_ref[rows, :], preferred_element_type=F32)

    @pl.when(jnp.logical_not(factorable))
    def _():
        row = lax.broadcasted_iota(jnp.int32, (c, c), 0)
        col = lax.broadcasted_iota(jnp.int32, (c, c), 1)

        def chunk(ci, _):
            rows = pl.ds(pl.multiple_of(ci * c, c), c)
            q = q_ref[rows, :].astype(F32)
            qf = q * jax.nn.sigmoid(q)
            gcum = g_s[rows, :]

            def column(s, a):
                src = pl.ds(ci * c + s, 1)
                term = qf * kf_s[src, :] * jnp.exp(jnp.minimum(gcum - g_s[src, :], 0.0))
                return jnp.where(col == s, jnp.sum(term, axis=1, keepdims=True), a)

            a = lax.fori_loop(0, c, column, jnp.zeros((c, c), F32))
            a = jnp.where(row >= col, a, 0.0).astype(BF16)
            intra_s[rows, :] = jnp.dot(a, i_ref[rows, :], preferred_element_type=F32)
            return 0

        lax.fori_loop(0, nchunks, chunk, 0)

    st = jnp.zeros((HEAD_W, HEAD_W), F32)
    for ci in range(nchunks):
        st_s[ci] = st.astype(BF16)
        st = st * jnp.broadcast_to(eg_s[ci][0:1, :], st.shape) + upd_s[ci]

    for gi in range(ngroups):
        rows = slice(gi * gr, (gi + 1) * gr)
        inter = jnp.concatenate(
            [lax.dot_general(qd_s[(gi * grp + j) * c:(gi * grp + j + 1) * c, :], st_s[gi * grp + j], nt,
                             preferred_element_type=F32) for j in range(grp)], axis=0)
        o = _rms(inter + intra_s[rows, :], gw)
        gate = g_ref[rows, :].astype(F32)
        o_ref[rows, :] = (o * (gate * jax.nn.sigmoid(gate))).astype(BF16)


def _hgrn2(proj, fl, lb_raw, gnorm_w, batch, seq, *, layer=0):
    t = proj.shape[0]
    heads = fl.shape[1] // HEAD_W
    gpt = PROJ_TN // HEAD_W
    depth1 = lb_raw.shape[0]
    assert seq % (REC_CHUNK * REC_GROUP) == 0
    nchunks = seq // REC_CHUNK
    blk = lambda col: pl.BlockSpec((seq, HEAD_W), lambda b, h: (b, col * gpt + h))
    return pl.pallas_call(
        functools.partial(_hgrn2_kernel, nchunks=nchunks, layer=layer),
        grid=(batch, heads),
        in_specs=[
            blk(COL_QR),
            pl.BlockSpec((seq, HEAD_W), lambda b, h: (b, h)),
            blk(COL_IR),
            blk(COL_GR),
            pl.BlockSpec((depth1, HEAD_W), lambda b, h: (0, h)),
            pl.BlockSpec((1, HEAD_W), lambda b, h: (0, 0)),
        ],
        out_specs=pl.BlockSpec((seq, HEAD_W), lambda b, h: (b, h)),
        out_shape=jax.ShapeDtypeStruct((t, heads * HEAD_W), BF16),
        scratch_shapes=[
            pltpu.VMEM((seq, HEAD_W), F32),
            pltpu.VMEM((seq, HEAD_W), F32),
            pltpu.VMEM((seq, HEAD_W), BF16),
            pltpu.VMEM((seq, HEAD_W), F32),
            pltpu.VMEM((nchunks, HEAD_W, HEAD_W), F32),
            pltpu.VMEM((nchunks, SUBLANES, HEAD_W), F32),
            pltpu.VMEM((nchunks, HEAD_W, HEAD_W), BF16),
        ],
        compiler_params=_params(("parallel", "parallel")),
        name="hgrn2",
    )(proj, fl, proj, proj, lb_raw.astype(F32), gnorm_w.reshape(1, -1).astype(F32))


def _merge_kernel(x_ref, ya_ref, yr_ref, ga0_ref, ga1_ref, gb0_ref, gb1_ref, wa_ref, wr_ref, wo_ref,
                  o_ref):
    pa = jnp.dot(ya_ref[...], wa_ref[...], preferred_element_type=F32)
    pr = jnp.dot(yr_ref[...], wr_ref[...], preferred_element_type=F32)
    ga = jnp.concatenate([ga0_ref[...], ga1_ref[...]], axis=1).astype(F32)
    gb = jnp.concatenate([gb0_ref[...], gb1_ref[...]], axis=1).astype(F32)
    merged = jax.nn.sigmoid(ga) * pa + jax.nn.sigmoid(gb) * pr
    o_ref[...] = x_ref[...] + jnp.dot(merged.astype(BF16), wo_ref[...], preferred_element_type=F32)


def _merge(x, ya, yr, proj, wa, wr, wo, *, tm=256):
    t, d = x.shape
    tm = min(tm, t)
    wa_k, wr_k = wa.shape[0], wr.shape[0]
    assert d == 2 * PROJ_TN
    const = lambda shape: pl.BlockSpec(shape, lambda i: (0, 0), pipeline_mode=pl.Buffered(1))
    gate = lambda col: pl.BlockSpec((tm, PROJ_TN), lambda i: (i, col))
    return pl.pallas_call(
        _merge_kernel,
        grid=(t // tm,),
        in_specs=[
            pl.BlockSpec((tm, d), lambda i: (i, 0)),
            pl.BlockSpec((tm, wa_k), lambda i: (i, 0)),
            pl.BlockSpec((tm, wr_k), lambda i: (i, 0)),
            gate(COL_GA), gate(COL_GA + 1), gate(COL_GB), gate(COL_GB + 1),
            const((wa_k, d)), const((wr_k, d)), const((d, d)),
        ],
        out_specs=pl.BlockSpec((tm, d), lambda i: (i, 0)),
        out_shape=jax.ShapeDtypeStruct((t, d), F32),
        compiler_params=_params(("parallel",)),
        name="merge",
    )(x, ya, yr, proj, proj, proj, proj, wa, wr, wo)


def kernel(x, ffn1_norm, ffn1_in, ffn1_out, mix_norm, w_in, lambda_q1, lambda_k1, lambda_q2,
           lambda_k2, attn_subln, rec_lb_raw, rec_gnorm, w_proj_attn, w_proj_rec, w_out,
           ffn2_norm, ffn2_in, ffn2_out, final_norm):
    batch, seq, d = x.shape
    depth = ffn1_in.shape[0]
    bf = lambda w: w.astype(BF16)
    xt = x.reshape(batch * seq, d)
    for l in range(depth):
        tm = min(FFN_TM, batch * seq)
        if batch * seq >= 2 * tm:
            head, f1_gate_b, f1_up_b, f1_out_b = _ffn_head(xt, ffn1_norm[l], ffn1_in[l], ffn1_out[l], tm=tm)
        else:
            head, f1_gate_b, f1_up_b, f1_out_b = None, bf(ffn1_in[l]), bf(ffn1_in[l]), bf(ffn1_out[l])
        xt, (w_in_b, f2_in_b, f2_out_b) = _ffn(
            xt, ffn1_norm[l], f1_gate_b, f1_up_b, f1_out_b, final_norm, final_norm=False, head=head,
            cast=(w_in[l], ffn2_in[l], ffn2_out[l]), tm=tm)
        proj, fl, (wa_b, wr_b, wo_b) = _proj(xt, mix_norm[l], w_in_b, seq,
                                             cast=(w_proj_attn[l], w_proj_rec[l], w_out[l]))
        ya = _attn(proj, lambda_q1[l], lambda_k1[l], lambda_q2[l], lambda_k2[l], attn_subln[l],
                   batch, seq, layer=l)
        yr = _hgrn2(proj, fl, rec_lb_raw, rec_gnorm[l], batch, seq, layer=l)
        xt = _merge(xt, ya, yr, proj, wa_b, wr_b, wo_b)
        xt, _ = _ffn(xt, ffn2_norm[l], f2_in_b, f2_in_b, f2_out_b, final_norm,
                     final_norm=(l == depth - 1), tm=tm)
    return xt.reshape(batch, seq, d)
```

```python
import functools
import math

import jax
import jax.numpy as jnp
from jax import lax
from jax.experimental import pallas as pl
from jax.experimental.pallas import tpu as pltpu

F32 = jnp.float32
BF16 = jnp.bfloat16

EPS = 1e-6
ATTN_HEADS = 8
ATTN_HEAD_DIM = 64
ROPE_THETA = 500000.0
ROT_DIM = ATTN_HEAD_DIM // 4
HEAD_W = 128
REC_CHUNK = 64
REC_GROUP = 4
REC_SAFE_SPAN = 80.0
LANES = 128
SUBLANES = 8
BF16_ROWS = 16
FFN_TM = 1024
PROJ_TN = 1024
COL_QA, COL_KA, COL_VA, COL_QR, COL_FR, COL_IR, COL_GR, COL_GA, COL_GB = 0, 1, 2, 3, 4, 5, 6, 7, 9
VMEM_LIMIT = 52 * 1024 * 1024
CAST_BLOCK_BYTES = 1024 * 1024


def _rms(x, w):
    return x * lax.rsqrt(jnp.mean(x * x, axis=-1, keepdims=True) + EPS) * w


def _params(sem):
    return pltpu.CompilerParams(dimension_semantics=sem, vmem_limit_bytes=VMEM_LIMIT)


def _row_tile_prefetch(x_hbm, xbuf, sem, consume, first=0):
    i, s = pl.program_id(0), pl.program_id(1)
    tm = xbuf.shape[0]

    def copy(tile):
        return pltpu.make_async_copy(x_hbm.at[pl.ds(pl.multiple_of(tile * tm, tm), tm), :], xbuf, sem)

    if first == 0:
        @pl.when((s == 0) & (i == 0))
        def _():
            copy(0).start()

    @pl.when((s == 0) & (i >= first))
    def _():
        copy(i).wait()
        consume(xbuf[...])

    @pl.when((s == 1) & (i + 1 >= first) & (i + 1 < pl.num_programs(0)))
    def _():
        copy(i + 1).start()


def _cast_plan(shape, ni, ninner):
    r, c = shape
    small = lambda rows, cols: rows * cols * 4 <= CAST_BLOCK_BYTES
    for nsteps, step in ((ni * ninner, lambda i, s: i * ninner + s), (ni, lambda i, s: i)):
        if r % nsteps == 0 and (r // nsteps) % BF16_ROWS == 0 and small(r // nsteps, c):
            return (r // nsteps, c), (lambda i, s, step=step: (step(i, s), 0))
        ncol = c // LANES
        if c % LANES == 0 and nsteps % ncol == 0:
            split = nsteps // ncol
            if r % split == 0 and (r // split) % BF16_ROWS == 0 and small(r // split, LANES):
                return ((r // split, LANES),
                        (lambda i, s, step=step, split=split: (step(i, s) % split, step(i, s) // split)))
    return None


def _cast_jobs(weights, ni, ninner):
    plans = [_cast_plan(w.shape, ni, ninner) for w in weights]
    ride = [(w, p) for w, p in zip(weights, plans) if p is not None]
    in_specs = [pl.BlockSpec(p[0], p[1]) for _, p in ride]
    out_shapes = [jax.ShapeDtypeStruct(w.shape, BF16) for w, _ in ride]
    return plans, [w for w, _ in ride], in_specs, out_shapes


def _merge_cast_results(weights, plans, cast_outs):
    outs = iter(cast_outs)
    return [next(outs) if p is not None else w.astype(BF16) for w, p in zip(weights, plans)]


def _swiglu_half(h, wg, wu, wo):
    g = jnp.dot(h, wg, preferred_element_type=F32)
    u = jnp.dot(h, wu, preferred_element_type=F32)
    a = (g * jax.nn.sigmoid(g) * u * 0.5).astype(BF16)
    return jnp.dot(a, wo, preferred_element_type=F32)


def _ffn_head_kernel(x_ref, nw_ref, wg_ref, wu_ref, wo_ref, o_ref, wg_b, wu_b, wo_b, h_ref):
    @pl.when(pl.program_id(0) == 0)
    def _():
        x = x_ref[...]
        h_ref[...] = _rms(x, nw_ref[...]).astype(BF16)
        o_ref[...] = x

    wg_b[...] = wg_ref[...].astype(BF16)
    wu_b[...] = wu_ref[...].astype(BF16)
    wo_b[...] = wo_ref[...].astype(BF16)
    o_ref[...] += _swiglu_half(h_ref[...], wg_b[...], wu_b[...], wo_b[...])


def _ffn_head(x, norm_w, w_in, w_out, *, tm, tf=256):
    t, d = x.shape
    ff = w_out.shape[0]
    nf = ff // tf
    assert ff % tf == 0 and t % tm == 0
    return pl.pallas_call(
        _ffn_head_kernel,
        grid=(nf,),
        in_specs=[
            pl.BlockSpec((tm, d), lambda f: (0, 0), pipeline_mode=pl.Buffered(1)),
            pl.BlockSpec((1, d), lambda f: (0, 0)),
            pl.BlockSpec((d, tf), lambda f: (0, f)),
            pl.BlockSpec((d, tf), lambda f: (0, f + nf)),
            pl.BlockSpec((tf, d), lambda f: (f, 0)),
        ],
        out_specs=[
            pl.BlockSpec((tm, d), lambda f: (0, 0)),
            pl.BlockSpec((d, tf), lambda f: (0, f)),
            pl.BlockSpec((d, tf), lambda f: (0, f)),
            pl.BlockSpec((tf, d), lambda f: (f, 0)),
        ],
        out_shape=[
            jax.ShapeDtypeStruct((tm, d), F32),
            jax.ShapeDtypeStruct((d, ff), BF16),
            jax.ShapeDtypeStruct((d, ff), BF16),
            jax.ShapeDtypeStruct((ff, d), BF16),
        ],
        scratch_shapes=[pltpu.VMEM((tm, d), BF16)],
        compiler_params=_params(("arbitrary",)),
        name="ffn_head",
    )(x, norm_w.reshape(1, d), w_in, w_in, w_out)


def _ffn_kernel(x_hbm, nw_ref, wg_ref, wu_ref, wo_ref, fw_ref, *rest, nf, ncast, has_head,
                final_norm):
    head_hbm = rest[0] if has_head else None
    rest = rest[has_head:]
    cast_in = rest[:ncast]
    o_ref = rest[ncast]
    cast_out = rest[ncast + 1:2 * ncast + 1]
    h_ref, xbuf, sem = rest[2 * ncast + 1:]
    i, f = pl.program_id(0), pl.program_id(1)
    for src, dst in zip(cast_in, cast_out):
        dst[...] = src[...].astype(BF16)

    def first_step(x):
        h = _rms(x, nw_ref[...]).astype(BF16)
        h_ref[...] = h
        o_ref[...] = x + _swiglu_half(h, wg_ref[...], wu_ref[...], wo_ref[...])

    _row_tile_prefetch(x_hbm, xbuf, sem, first_step, first=int(has_head))

    @pl.when((f > 0) & (i >= int(has_head)))
    def _():
        o_ref[...] += _swiglu_half(h_ref[...], wg_ref[...], wu_ref[...], wo_ref[...])

    if has_head:
        @pl.when((i == 0) & (f == 0))
        def _():
            copy = pltpu.make_async_copy(head_hbm, o_ref, sem)
            copy.start()
            copy.wait()

    if final_norm:
        @pl.when(f == nf - 1)
        def _():
            o_ref[...] = _rms(o_ref[...], fw_ref[...])


def _ffn(x, norm_w, w_gate, w_up, w_out, final_w, *, final_norm, head=None, cast=(), tm=1024, tf=512):
    t, d = x.shape
    ff = w_out.shape[0]
    tm, tf = min(tm, t), min(tf, ff)
    nf = ff // tf
    assert t % tm == 0 and ff % tf == 0 and nf >= 2
    assert head is None or (head.shape == (tm, d) and t // tm >= 2 and not final_norm)
    up0 = nf if w_up.shape[1] == 2 * ff else 0
    hold = (lambda i, f: f) if head is None else (lambda i, f: jnp.where(i == 0, 0, f))
    plans, cast_w, cast_specs, cast_shapes = _cast_jobs(cast, t // tm, nf)
    outs = pl.pallas_call(
        functools.partial(_ffn_kernel, nf=nf, ncast=len(cast_w), has_head=head is not None,
                          final_norm=final_norm),
        grid=(t // tm, nf),
        in_specs=[
            pl.BlockSpec(memory_space=pl.ANY),
            pl.BlockSpec((1, d), lambda i, f: (0, 0)),
            pl.BlockSpec((d, tf), lambda i, f: (0, hold(i, f))),
            pl.BlockSpec((d, tf), lambda i, f: (0, hold(i, f) + up0)),
            pl.BlockSpec((tf, d), lambda i, f: (hold(i, f), 0)),
            pl.BlockSpec((1, d), lambda i, f: (0, 0)),
        ] + ([pl.BlockSpec(memory_space=pl.ANY)] if head is not None else []) + cast_specs,
        out_specs=[pl.BlockSpec((tm, d), lambda i, f: (i, 0))] + cast_specs,
        out_shape=[jax.ShapeDtypeStruct((t, d), F32)] + cast_shapes,
        scratch_shapes=[pltpu.VMEM((tm, d), BF16), pltpu.VMEM((tm, d), F32),
                        pltpu.SemaphoreType.DMA(())],
        compiler_params=_params(("arbitrary", "arbitrary")),
        name="ffn",
    )(x, norm_w.reshape(1, d), w_gate, w_up, w_out, final_w.reshape(1, d),
      *([head] if head is not None else []), *cast_w)
    return outs[0], _merge_cast_results(cast, plans, outs[1:])


def _proj_kernel(x_hbm, nw_ref, w_ref, cos_ref, sina_ref, sinb_ref, *rest, ncast):
    cast_in = rest[:ncast]
    o_ref, fl_ref = rest[ncast:ncast + 2]
    cast_out = rest[ncast + 2:2 * ncast + 2]
    h_ref, xbuf, sem = rest[2 * ncast + 2:]
    j = pl.program_id(1)
    for src, dst in zip(cast_in, cast_out):
        dst[...] = src[...].astype(BF16)

    def project(h):
        return jnp.dot(h, w_ref[...], preferred_element_type=F32)

    def store_rotated(r, scale):
        half = ROT_DIM // 2
        cos, sina, sinb = (t[...] if scale == 1.0 else t[...] * scale
                           for t in (cos_ref, sina_ref, sinb_ref))
        for c0 in range(0, r.shape[1], LANES):
            rc = r[:, c0:c0 + LANES]
            rot = rc * cos + pltpu.roll(rc, LANES - half, 1) * sina + pltpu.roll(rc, half, 1) * sinb
            o_ref[:, c0:c0 + LANES] = rot.astype(BF16)

    def first_step(x):
        h = _rms(x, nw_ref[...]).astype(BF16)
        h_ref[...] = h
        store_rotated(project(h), ATTN_HEAD_DIM ** -0.5 * math.log2(math.e))

    assert COL_QA == 0
    _row_tile_prefetch(x_hbm, xbuf, sem, first_step)

    @pl.when(j == COL_KA)
    def _():
        store_rotated(project(h_ref[...]), 1.0)

    @pl.when(j == COL_FR)
    def _():
        r = project(h_ref[...])
        fl_ref[...] = r
        o_ref[...] = r.astype(BF16)

    @pl.when((j > COL_KA) & (j != COL_FR))
    def _():
        o_ref[...] = project(h_ref[...]).astype(BF16)


def _rope_tables(seq):
    half = ROT_DIM // 2
    inv_freq = ROPE_THETA ** (-jnp.arange(0, ROT_DIM, 2, dtype=F32) / ROT_DIM)
    ang = jnp.arange(seq).astype(F32)[:, None] * inv_freq[None, :]
    cos8, sin8 = jnp.cos(ang), jnp.sin(ang)
    pad = ATTN_HEAD_DIM - ROT_DIM
    ones = jnp.ones((seq, pad), F32)
    zeros = jnp.zeros((seq, pad), F32)
    z8 = jnp.zeros((seq, half), F32)
    cos = jnp.concatenate([cos8, cos8, ones], axis=1)
    sina = jnp.concatenate([-sin8, z8, zeros], axis=1)
    sinb = jnp.concatenate([z8, sin8, zeros], axis=1)
    rep = LANES // ATTN_HEAD_DIM
    return tuple(jnp.tile(a, (1, rep)) for a in (cos, sina, sinb))


def _proj(x, norm_w, w, seq, *, cast=(), tm=1024):
    t, d = x.shape
    n = w.shape[1]
    tm = min(tm, seq)
    tn = PROJ_TN
    assert t % tm == 0 and seq % tm == 0 and n % tn == 0
    spt = seq // tm
    cos, sina, sinb = _rope_tables(seq)
    tab_spec = pl.BlockSpec((tm, LANES), lambda i, j: (i % spt, 0))
    plans, cast_w, cast_specs, cast_shapes = _cast_jobs(cast, t // tm, n // tn)
    outs = pl.pallas_call(
        functools.partial(_proj_kernel, ncast=len(cast_w)),
        grid=(t // tm, n // tn),
        in_specs=[
            pl.BlockSpec(memory_space=pl.ANY),
            pl.BlockSpec((1, d), lambda i, j: (0, 0)),
            pl.BlockSpec((d, tn), lambda i, j: (0, j)),
            tab_spec, tab_spec, tab_spec,
        ] + cast_specs,
        out_specs=[
            pl.BlockSpec((tm, tn), lambda i, j: (i, j)),
            pl.BlockSpec((tm, tn), lambda i, j: (i, 0)),
        ] + cast_specs,
        out_shape=[
            jax.ShapeDtypeStruct((t, n), BF16),
            jax.ShapeDtypeStruct((t, tn), F32),
        ] + cast_shapes,
        scratch_shapes=[pltpu.VMEM((tm, d), BF16), pltpu.VMEM((tm, d), F32),
                        pltpu.SemaphoreType.DMA(())],
        compiler_params=_params(("arbitrary", "arbitrary")),
        name="proj",
    )(x, norm_w.reshape(1, d), w, cos, sina, sinb, *cast_w)
    return outs[0], outs[1], _merge_cast_results(cast, plans, outs[2:])


def _attn_kernel(q_ref, k_ref, v_ref, lq1_ref, lk1_ref, lq2_ref, lk2_ref, sw_ref, o_ref,
                 vx_ref, *, tq, lambda_init):
    hd = ATTN_HEAD_DIM
    w = HEAD_W
    hq = tq // 2
    seq = q_ref.shape[0]
    nt = (((1,), (1,)), ((), ()))
    vx_ref[:, 0:w] = v_ref[...]
    vx_ref[:, w:2 * w] = jnp.ones(v_ref.shape, BF16)
    lam = (jnp.exp(jnp.sum(lq1_ref[...] * lk1_ref[...], axis=-1, keepdims=True))
           - jnp.exp(jnp.sum(lq2_ref[...] * lk2_ref[...], axis=-1, keepdims=True))
           + lambda_init)
    lane = lax.broadcasted_iota(jnp.int32, (hq, w), 1)
    row = lax.broadcasted_iota(jnp.int32, (hq, hq), 0)
    col = lax.broadcasted_iota(jnp.int32, (hq, hq), 1)
    tri = row >= col
    tri2 = jnp.concatenate([tri, tri], axis=0)
    tri2_all = jnp.concatenate([tri2, jnp.ones((2 * hq, hq), jnp.bool_)], axis=0)

    def update(m, acc, s, vx):
        m_new = jnp.maximum(m, jnp.max(s, axis=-1, keepdims=True))
        alpha = jnp.exp2(m - m_new)
        p = jnp.exp2(s - jnp.concatenate([m_new] * (s.shape[1] // w), axis=1))
        acc = (jnp.concatenate([alpha, alpha], axis=1) * acc
               + jnp.dot(p.astype(BF16), vx, preferred_element_type=F32))
        return m_new, acc

    for qi in range(seq // tq):
        parts = []
        for r0 in (qi * tq, qi * tq + hq):
            q = q_ref[r0:r0 + hq, :]
            zero = jnp.zeros_like(q)
            parts += [jnp.where(lane < hd, q, zero), jnp.where(lane >= hd, q, zero)]
        qs = jnp.concatenate(parts, axis=0)
        m = jnp.full((2 * tq, w), -jnp.inf, F32)
        acc = jnp.zeros((2 * tq, 2 * w), F32)
        for kb in range(qi):
            keys = slice(kb * tq, (kb + 1) * tq)
            s = lax.dot_general(qs, k_ref[keys, :], nt, preferred_element_type=F32)
            m, acc = update(m, acc, s, vx_ref[keys, :])
        keys = slice(qi * tq, qi * tq + hq)
        s = lax.dot_general(qs, k_ref[keys, :], nt, preferred_element_type=F32)
        m, acc = update(m, acc, jnp.where(tri2_all, s, -jnp.inf), vx_ref[keys, :])
        keys = slice(qi * tq + hq, (qi + 1) * tq)
        s = lax.dot_general(qs[tq:, :], k_ref[keys, :], nt, preferred_element_type=F32)
        m_hi, acc_hi = update(m[tq:, :], acc[tq:, :], jnp.where(tri2, s, -jnp.inf), vx_ref[keys, :])
        acc = jnp.concatenate([acc[:tq, :], acc_hi], axis=0)
        o = acc[:, 0:w] / acc[:, w:2 * w]
        o = jnp.concatenate([o[0:hq, :] - lam * o[hq:tq, :],
                             o[tq:tq + hq, :] - lam * o[tq + hq:2 * tq, :]], axis=0)
        o_ref[qi * tq:(qi + 1) * tq, :] = (_rms(o, sw_ref[...]) * (1.0 - lambda_init)).astype(BF16)


def _attn(proj, lq1, lk1, lq2, lk2, subln_w, batch, seq, *, layer=0, tq=512):
    t = proj.shape[0]
    tq = min(tq, seq)
    assert seq % tq == 0 and tq % HEAD_W == 0
    heads = ATTN_HEADS
    gpt = PROJ_TN // HEAD_W
    lambda_init = 0.8 - 0.6 * math.exp(-0.3 * layer)
    vec = lambda a: a.reshape(1, -1).astype(F32)
    small = lambda w: pl.BlockSpec((1, w), lambda b, h: (0, 0))
    blk = lambda col: pl.BlockSpec((seq, HEAD_W), lambda b, h: (b, col * gpt + h))
    return pl.pallas_call(
        functools.partial(_attn_kernel, tq=tq, lambda_init=lambda_init),
        grid=(batch, heads),
        in_specs=[
            blk(COL_QA), blk(COL_KA), blk(COL_VA),
            small(ATTN_HEAD_DIM), small(ATTN_HEAD_DIM), small(ATTN_HEAD_DIM), small(ATTN_HEAD_DIM),
            small(HEAD_W),
        ],
        out_specs=pl.BlockSpec((seq, HEAD_W), lambda b, h: (b, h)),
        out_shape=jax.ShapeDtypeStruct((t, heads * HEAD_W), BF16),
        scratch_shapes=[pltpu.VMEM((seq, 2 * HEAD_W), BF16)],
        compiler_params=_params(("parallel", "parallel")),
        name="attn",
    )(proj, proj, proj, vec(lq1), vec(lk1), vec(lq2), vec(lk2), vec(subln_w))


def _hgrn2_kernel(q_ref, fl_ref, i_ref, g_ref, lbraw_ref, gw_ref, o_ref,
                  g_s, kf_s, qd_s, intra_s, upd_s, eg_s, st_s, *, nchunks, layer):
    c = REC_CHUNK
    nt = (((1,), (1,)), ((), ()))
    tn = (((0,), (0,)), ((), ()))
    raw = lbraw_ref[...]
    e = jnp.exp(raw - jnp.max(raw, axis=0, keepdims=True))
    lb = jnp.sum(e[0:layer + 1], axis=0, keepdims=True) / jnp.sum(e, axis=0, keepdims=True)
    gw = gw_ref[...]
    seq = nchunks * c
    grp = REC_GROUP
    gr = grp * c
    ngroups = nchunks // grp

    f = lb + (1.0 - lb) * jax.nn.sigmoid(fl_ref[...])
    kf_s[...] = 1.0 - f
    g = jnp.log(f)
    pos = lax.broadcasted_iota(jnp.int32, (seq, HEAD_W), 0) % c
    shift = 1
    while shift < c:
        g = g + jnp.where(pos >= shift, pltpu.roll(g, shift, 0), 0.0)
        shift *= 2
    g_s[...] = g
    factorable = jnp.min(g) >= -REC_SAFE_SPAN

    def chunk_last(gcum):
        n = gcum.shape[0] // c
        return jnp.concatenate(
            [jnp.broadcast_to(gcum[(j + 1) * c - 1:(j + 1) * c, :], (c, HEAD_W)) for j in range(n)],
            axis=0)

    for gi in range(ngroups):
        rows = slice(gi * gr, (gi + 1) * gr)
        q = q_ref[rows, :].astype(F32)
        gcum = g_s[rows, :]
        qd_s[rows, :] = (q * jax.nn.sigmoid(q) * jnp.exp(gcum)).astype(BF16)
        glast = chunk_last(gcum)
        kd = (kf_s[rows, :] * jnp.exp(glast - gcum)).astype(BF16)
        v = i_ref[rows, :]
        for j in range(grp):
            ci = gi * grp + j
            sub = slice(j * c, (j + 1) * c)
            upd_s[ci] = lax.dot_general(v[sub, :], kd[sub, :], tn, preferred_element_type=F32)
            eg_s[ci] = jnp.exp(glast[j * c:j * c + SUBLANES, :])

    @pl.when(factorable)
    def _():
        row = lax.broadcasted_iota(jnp.int32, (gr, gr), 0)
        col = lax.broadcasted_iota(jnp.int32, (gr, gr), 1)
        keep = (row >= col) & ((row // c) == (col // c))
        for gi in range(ngroups):
            rows = slice(gi * gr, (gi + 1) * gr)
            ku = (kf_s[rows, :] * jnp.exp(-g_s[rows, :])).astype(BF16)
            a = lax.dot_general(qd_s[rows, :], ku, nt, preferred_element_type=F32)
            a = jnp.where(keep, a, 0.0).astype(BF16)
            intra_s[rows, :] = jnp.dot(a, i_ref[rows, :], preferred_element_type=F32)

    @pl.when(jnp.logical_not(factorable))
    def _():
        row = lax.broadcasted_iota(jnp.int32, (c, c), 0)
        col = lax.broadcasted_iota(jnp.int32, (c, c), 1)

        def chunk(ci, _):
            rows = pl.ds(pl.multiple_of(ci * c, c), c)
            q = q_ref[rows, :].astype(F32)
            qf = q * jax.nn.sigmoid(q)
            gcum = g_s[rows, :]

            def column(s, a):
                src = pl.ds(ci * c + s, 1)
                term = qf * kf_s[src, :] * jnp.exp(jnp.minimum(gcum - g_s[src, :], 0.0))
                return jnp.where(col == s, jnp.sum(term, axis=1, keepdims=True), a)

            a = lax.fori_loop(0, c, column, jnp.zeros((c, c), F32))
            a = jnp.where(row >= col, a, 0.0).astype(BF16)
            intra_s[rows, :] = jnp.dot(a, i_ref[rows, :], preferred_element_type=F32)
            return 0

        lax.fori_loop(0, nchunks, chunk, 0)

    st = jnp.zeros((HEAD_W, HEAD_W), F32)
    for ci in range(nchunks):
        st_s[ci] = st.astype(BF16)
        st = st * jnp.broadcast_to(eg_s[ci][0:1, :], st.shape) + upd_s[ci]

    for gi in range(ngroups):
        rows = slice(gi * gr, (gi + 1) * gr)
        inter = jnp.concatenate(
            [lax.dot_general(qd_s[(gi * grp + j) * c:(gi * grp + j + 1) * c, :], st_s[gi * grp + j], nt,
                             preferred_element_type=F32) for j in range(grp)], axis=0)
        o = _rms(inter + intra_s[rows, :], gw)
        gate = g_ref[rows, :].astype(F32)
        o_ref[rows, :] = (o * (gate * jax.nn.sigmoid(gate))).astype(BF16)


def _hgrn2(proj, fl, lb_raw, gnorm_w, batch, seq, *, layer=0):
    t = proj.shape[0]
    heads = fl.shape[1] // HEAD_W
    gpt = PROJ_TN // HEAD_W
    depth1 = lb_raw.shape[0]
    assert seq % (REC_CHUNK * REC_GROUP) == 0
    nchunks = seq // REC_CHUNK
    blk = lambda col: pl.BlockSpec((seq, HEAD_W), lambda b, h: (b, col * gpt + h))
    return pl.pallas_call(
        functools.partial(_hgrn2_kernel, nchunks=nchunks, layer=layer),
        grid=(batch, heads),
        in_specs=[
            blk(COL_QR),
            pl.BlockSpec((seq, HEAD_W), lambda b, h: (b, h)),
            blk(COL_IR),
            blk(COL_GR),
            pl.BlockSpec((depth1, HEAD_W), lambda b, h: (0, h)),
            pl.BlockSpec((1, HEAD_W), lambda b, h: (0, 0)),
        ],
        out_specs=pl.BlockSpec((seq, HEAD_W), lambda b, h: (b, h)),
        out_shape=jax.ShapeDtypeStruct((t, heads * HEAD_W), BF16),
        scratch_shapes=[
            pltpu.VMEM((seq, HEAD_W), F32),
            pltpu.VMEM((seq, HEAD_W), F32),
            pltpu.VMEM((seq, HEAD_W), BF16),
            pltpu.VMEM((seq, HEAD_W), F32),
            pltpu.VMEM((nchunks, HEAD_W, HEAD_W), F32),
            pltpu.VMEM((nchunks, SUBLANES, HEAD_W), F32),
            pltpu.VMEM((nchunks, HEAD_W, HEAD_W), BF16),
        ],
        compiler_params=_params(("parallel", "parallel")),
        name="hgrn2",
    )(proj, fl, proj, proj, lb_raw.astype(F32), gnorm_w.reshape(1, -1).astype(F32))


def _merge_kernel(x_ref, ya_ref, yr_ref, ga0_ref, ga1_ref, gb0_ref, gb1_ref, wa_ref, wr_ref, wo_ref,
                  o_ref):
    pa = jnp.dot(ya_ref[...], wa_ref[...], preferred_element_type=F32)
    pr = jnp.dot(yr_ref[...], wr_ref[...], preferred_element_type=F32)
    ga = jnp.concatenate([ga0_ref[...], ga1_ref[...]], axis=1).astype(F32)
    gb = jnp.concatenate([gb0_ref[...], gb1_ref[...]], axis=1).astype(F32)
    merged = jax.nn.sigmoid(ga) * pa + jax.nn.sigmoid(gb) * pr
    o_ref[...] = x_ref[...] + jnp.dot(merged.astype(BF16), wo_ref[...], preferred_element_type=F32)


def _merge(x, ya, yr, proj, wa, wr, wo, *, tm=256):
    t, d = x.shape
    tm = min(tm, t)
    wa_k, wr_k = wa.shape[0], wr.shape[0]
    assert d == 2 * PROJ_TN
    const = lambda shape: pl.BlockSpec(shape, lambda i: (0, 0), pipeline_mode=pl.Buffered(1))
    gate = lambda col: pl.BlockSpec((tm, PROJ_TN), lambda i: (i, col))
    return pl.pallas_call(
        _merge_kernel,
        grid=(t // tm,),
        in_specs=[
            pl.BlockSpec((tm, d), lambda i: (i, 0)),
            pl.BlockSpec((tm, wa_k), lambda i: (i, 0)),
            pl.BlockSpec((tm, wr_k), lambda i: (i, 0)),
            gate(COL_GA), gate(COL_GA + 1), gate(COL_GB), gate(COL_GB + 1),
            const((wa_k, d)), const((wr_k, d)), const((d, d)),
        ],
        out_specs=pl.BlockSpec((tm, d), lambda i: (i, 0)),
        out_shape=jax.ShapeDtypeStruct((t, d), F32),
        compiler_params=_params(("parallel",)),
        name="merge",
    )(x, ya, yr, proj, proj, proj, proj, wa, wr, wo)


def kernel(x, ffn1_norm, ffn1_in, ffn1_out, mix_norm, w_in, lambda_q1, lambda_k1, lambda_q2,
           lambda_k2, attn_subln, rec_lb_raw, rec_gnorm, w_proj_attn, w_proj_rec, w_out,
           ffn2_norm, ffn2_in, ffn2_out, final_norm):
    batch, seq, d = x.shape
    depth = ffn1_in.shape[0]
    bf = lambda w: w.astype(BF16)
    xt = x.reshape(batch * seq, d)
    for l in range(depth):
        tm = min(FFN_TM, batch * seq)
        if batch * seq >= 2 * tm:
            head, f1_gate_b, f1_up_b, f1_out_b = _ffn_head(xt, ffn1_norm[l], ffn1_in[l], ffn1_out[l], tm=tm)
        else:
            head, f1_gate_b, f1_up_b, f1_out_b = None, bf(ffn1_in[l]), bf(ffn1_in[l]), bf(ffn1_out[l])
        xt, (w_in_b, f2_in_b, f2_out_b) = _ffn(
            xt, ffn1_norm[l], f1_gate_b, f1_up_b, f1_out_b, final_norm, final_norm=False, head=head,
            cast=(w_in[l], ffn2_in[l], ffn2_out[l]), tm=tm)
        proj, fl, (wa_b, wr_b, wo_b) = _proj(xt, mix_norm[l], w_in_b, seq,
                                             cast=(w_proj_attn[l], w_proj_rec[l], w_out[l]))
        ya = _attn(proj, lambda_q1[l], lambda_k1[l], lambda_q2[l], lambda_k2[l], attn_subln[l],
                   batch, seq, layer=l)
        yr = _hgrn2(proj, fl, rec_lb_raw, rec_gnorm[l], batch, seq, layer=l)
        xt = _merge(xt, ya, yr, proj, wa_b, wr_b, wo_b)
        xt, _ = _ffn(xt, ffn2_norm[l], f2_in_b, f2_in_b, f2_out_b, final_norm,
                     final_norm=(l == depth - 1), tm=tm)
    return xt.reshape(batch, seq, d)
```

```python
import functools
import math

import jax
import jax.numpy as jnp
from jax import lax
from jax.experimental import pallas as pl
from jax.experimental.pallas import tpu as pltpu

F32 = jnp.float32
BF16 = jnp.bfloat16

EPS = 1e-6
ATTN_HEADS = 8
ATTN_HEAD_DIM = 64
ROPE_THETA = 500000.0
ROT_DIM = ATTN_HEAD_DIM // 4
HEAD_W = 128
REC_CHUNK = 64
REC_GROUP = 4
REC_SAFE_SPAN_LOG2 = 115.0
LANES = 128
SUBLANES = 8
BF16_ROWS = 16
FFN_TM = 1024
PROJ_TN = 1024
COL_QA, COL_KA, COL_VA, COL_QR, COL_FR, COL_IR, COL_GR, COL_GA, COL_GB = 0, 1, 2, 3, 4, 5, 6, 7, 9
VMEM_LIMIT = 52 * 1024 * 1024
CAST_BLOCK_BYTES = 1024 * 1024


def _rms(x, w):
    return x * lax.rsqrt(jnp.mean(x * x, axis=-1, keepdims=True) + EPS) * w


def _params(sem):
    return pltpu.CompilerParams(dimension_semantics=sem, vmem_limit_bytes=VMEM_LIMIT)


def _row_tile_prefetch(x_hbm, xbuf, sem, consume, first=0):
    i, s = pl.program_id(0), pl.program_id(1)
    tm = xbuf.shape[0]

    def copy(tile):
        return pltpu.make_async_copy(x_hbm.at[pl.ds(pl.multiple_of(tile * tm, tm), tm), :], xbuf, sem)

    if first == 0:
        @pl.when((s == 0) & (i == 0))
        def _():
            copy(0).start()

    @pl.when((s == 0) & (i >= first))
    def _():
        copy(i).wait()
        consume(xbuf[...])

    @pl.when((s == 1) & (i + 1 >= first) & (i + 1 < pl.num_programs(0)))
    def _():
        copy(i + 1).start()


def _cast_plan(shape, ni, ninner):
    r, c = shape
    small = lambda rows, cols: rows * cols * 4 <= CAST_BLOCK_BYTES
    for nsteps, step in ((ni * ninner, lambda i, s: i * ninner + s), (ni, lambda i, s: i)):
        if r % nsteps == 0 and (r // nsteps) % BF16_ROWS == 0 and small(r // nsteps, c):
            return (r // nsteps, c), (lambda i, s, step=step: (step(i, s), 0))
        ncol = c // LANES
        if c % LANES == 0 and nsteps % ncol == 0:
            split = nsteps // ncol
            if r % split == 0 and (r // split) % BF16_ROWS == 0 and small(r // split, LANES):
                return ((r // split, LANES),
                        (lambda i, s, step=step, split=split: (step(i, s) % split, step(i, s) // split)))
    return None


def _cast_jobs(weights, ni, ninner):
    plans = [_cast_plan(w.shape, ni, ninner) for w in weights]
    ride = [(w, p) for w, p in zip(weights, plans) if p is not None]
    in_specs = [pl.BlockSpec(p[0], p[1]) for _, p in ride]
    out_shapes = [jax.ShapeDtypeStruct(w.shape, BF16) for w, _ in ride]
    return plans, [w for w, _ in ride], in_specs, out_shapes


def _merge_cast_results(weights, plans, cast_outs):
    outs = iter(cast_outs)
    return [next(outs) if p is not None else w.astype(BF16) for w, p in zip(weights, plans)]


def _swiglu_half(h, wg, wu, wo):
    g = jnp.dot(h, wg, preferred_element_type=F32)
    u = jnp.dot(h, wu, preferred_element_type=F32)
    a = (g * jax.nn.sigmoid(g) * u * 0.5).astype(BF16)
    return jnp.dot(a, wo, preferred_element_type=F32)


def _ffn_head_kernel(x_ref, nw_ref, wg_ref, wu_ref, wo_ref, o_ref, wg_b, wu_b, wo_b, h_ref):
    @pl.when(pl.program_id(0) == 0)
    def _():
        x = x_ref[...]
        h_ref[...] = _rms(x, nw_ref[...]).astype(BF16)
        o_ref[...] = x

    wg_b[...] = wg_ref[...].astype(BF16)
    wu_b[...] = wu_ref[...].astype(BF16)
    wo_b[...] = wo_ref[...].astype(BF16)
    o_ref[...] += _swiglu_half(h_ref[...], wg_b[...], wu_b[...], wo_b[...])


def _ffn_head(x, norm_w, w_in, w_out, *, tm, tf=256):
    t, d = x.shape
    ff = w_out.shape[0]
    nf = ff // tf
    assert ff % tf == 0 and t % tm == 0
    return pl.pallas_call(
        _ffn_head_kernel,
        grid=(nf,),
        in_specs=[
            pl.BlockSpec((tm, d), lambda f: (0, 0), pipeline_mode=pl.Buffered(1)),
            pl.BlockSpec((1, d), lambda f: (0, 0)),
            pl.BlockSpec((d, tf), lambda f: (0, f)),
            pl.BlockSpec((d, tf), lambda f: (0, f + nf)),
            pl.BlockSpec((tf, d), lambda f: (f, 0)),
        ],
        out_specs=[
            pl.BlockSpec((tm, d), lambda f: (0, 0)),
            pl.BlockSpec((d, tf), lambda f: (0, f)),
            pl.BlockSpec((d, tf), lambda f: (0, f)),
            pl.BlockSpec((tf, d), lambda f: (f, 0)),
        ],
        out_shape=[
            jax.ShapeDtypeStruct((tm, d), F32),
            jax.ShapeDtypeStruct((d, ff), BF16),
            jax.ShapeDtypeStruct((d, ff), BF16),
            jax.ShapeDtypeStruct((ff, d), BF16),
        ],
        scratch_shapes=[pltpu.VMEM((tm, d), BF16)],
        compiler_params=_params(("arbitrary",)),
        name="ffn_head",
    )(x, norm_w.reshape(1, d), w_in, w_in, w_out)


def _ffn_kernel(x_hbm, nw_ref, wg_ref, wu_ref, wo_ref, fw_ref, *rest, nf, ncast, has_head,
                final_norm):
    head_hbm = rest[0] if has_head else None
    rest = rest[has_head:]
    cast_in = rest[:ncast]
    o_ref = rest[ncast]
    cast_out = rest[ncast + 1:2 * ncast + 1]
    h_ref, xbuf, sem = rest[2 * ncast + 1:]
    i, f = pl.program_id(0), pl.program_id(1)
    for src, dst in zip(cast_in, cast_out):
        dst[...] = src[...].astype(BF16)

    def first_step(x):
        h = _rms(x, nw_ref[...]).astype(BF16)
        h_ref[...] = h
        o_ref[...] = x + _swiglu_half(h, wg_ref[...], wu_ref[...], wo_ref[...])

    _row_tile_prefetch(x_hbm, xbuf, sem, first_step, first=int(has_head))

    last = nf - 1 if final_norm else nf

    @pl.when((f > 0) & (f < last) & (i >= int(has_head)))
    def _():
        o_ref[...] += _swiglu_half(h_ref[...], wg_ref[...], wu_ref[...], wo_ref[...])

    if has_head:
        @pl.when((i == 0) & (f == 0))
        def _():
            copy = pltpu.make_async_copy(head_hbm, o_ref, sem)
            copy.start()
            copy.wait()

    if final_norm:
        @pl.when(f == nf - 1)
        def _():
            o = o_ref[...] + _swiglu_half(h_ref[...], wg_ref[...], wu_ref[...], wo_ref[...])
            o_ref[...] = _rms(o, fw_ref[...])


def _ffn(x, norm_w, w_gate, w_up, w_out, final_w, *, final_norm, head=None, cast=(), tm=1024, tf=512):
    t, d = x.shape
    ff = w_out.shape[0]
    tm, tf = min(tm, t), min(tf, ff)
    nf = ff // tf
    assert t % tm == 0 and ff % tf == 0 and nf >= 2
    assert head is None or (head.shape == (tm, d) and t // tm >= 2 and not final_norm)
    up0 = nf if w_up.shape[1] == 2 * ff else 0
    hold = (lambda i, f: f) if head is None else (lambda i, f: jnp.where(i == 0, 0, f))
    plans, cast_w, cast_specs, cast_shapes = _cast_jobs(cast, t // tm, nf)
    outs = pl.pallas_call(
        functools.partial(_ffn_kernel, nf=nf, ncast=len(cast_w), has_head=head is not None,
                          final_norm=final_norm),
        grid=(t // tm, nf),
        in_specs=[
            pl.BlockSpec(memory_space=pl.ANY),
            pl.BlockSpec((1, d), lambda i, f: (0, 0)),
            pl.BlockSpec((d, tf), lambda i, f: (0, hold(i, f))),
            pl.BlockSpec((d, tf), lambda i, f: (0, hold(i, f) + up0)),
            pl.BlockSpec((tf, d), lambda i, f: (hold(i, f), 0)),
            pl.BlockSpec((1, d), lambda i, f: (0, 0)),
        ] + ([pl.BlockSpec(memory_space=pl.ANY)] if head is not None else []) + cast_specs,
        out_specs=[pl.BlockSpec((tm, d), lambda i, f: (i, 0))] + cast_specs,
        out_shape=[jax.ShapeDtypeStruct((t, d), F32)] + cast_shapes,
        scratch_shapes=[pltpu.VMEM((tm, d), BF16), pltpu.VMEM((tm, d), F32),
                        pltpu.SemaphoreType.DMA(())],
        compiler_params=_params(("arbitrary", "arbitrary")),
        name="ffn",
    )(x, norm_w.reshape(1, d), w_gate, w_up, w_out, final_w.reshape(1, d),
      *([head] if head is not None else []), *cast_w)
    return outs[0], _merge_cast_results(cast, plans, outs[1:])


def _proj_kernel(x_hbm, nw_ref, w_ref, cos_ref, sina_ref, sinb_ref, *rest, ncast):
    cast_in = rest[:ncast]
    o_ref, fl_ref = rest[ncast:ncast + 2]
    cast_out = rest[ncast + 2:2 * ncast + 2]
    h_ref, xbuf, sem = rest[2 * ncast + 2:]
    j = pl.program_id(1)
    for src, dst in zip(cast_in, cast_out):
        dst[...] = src[...].astype(BF16)

    def project(h):
        return jnp.dot(h, w_ref[...], preferred_element_type=F32)

    def store_rotated(r, scale):
        half = ROT_DIM // 2
        cos, sina, sinb = (t[...] if scale == 1.0 else t[...] * scale
                           for t in (cos_ref, sina_ref, sinb_ref))
        for c0 in range(0, r.shape[1], LANES):
            rc = r[:, c0:c0 + LANES]
            rot = rc * cos + pltpu.roll(rc, LANES - half, 1) * sina + pltpu.roll(rc, half, 1) * sinb
            o_ref[:, c0:c0 + LANES] = rot.astype(BF16)

    def first_step(x):
        h = _rms(x, nw_ref[...]).astype(BF16)
        h_ref[...] = h
        store_rotated(project(h), ATTN_HEAD_DIM ** -0.5 * math.log2(math.e))

    assert COL_QA == 0
    _row_tile_prefetch(x_hbm, xbuf, sem, first_step)

    @pl.when(j == COL_KA)
    def _():
        store_rotated(project(h_ref[...]), 1.0)

    @pl.when(j == COL_FR)
    def _():
        r = project(h_ref[...])
        fl_ref[...] = r
        o_ref[...] = r.astype(BF16)

    @pl.when((j > COL_KA) & (j != COL_FR))
    def _():
        o_ref[...] = project(h_ref[...]).astype(BF16)


def _rope_tables(seq):
    half = ROT_DIM // 2
    inv_freq = ROPE_THETA ** (-jnp.arange(0, ROT_DIM, 2, dtype=F32) / ROT_DIM)
    ang = jnp.arange(seq).astype(F32)[:, None] * inv_freq[None, :]
    cos8, sin8 = jnp.cos(ang), jnp.sin(ang)
    pad = ATTN_HEAD_DIM - ROT_DIM
    ones = jnp.ones((seq, pad), F32)
    zeros = jnp.zeros((seq, pad), F32)
    z8 = jnp.zeros((seq, half), F32)
    cos = jnp.concatenate([cos8, cos8, ones], axis=1)
    sina = jnp.concatenate([-sin8, z8, zeros], axis=1)
    sinb = jnp.concatenate([z8, sin8, zeros], axis=1)
    rep = LANES // ATTN_HEAD_DIM
    return tuple(jnp.tile(a, (1, rep)) for a in (cos, sina, sinb))


def _proj(x, norm_w, w, seq, *, cast=(), tm=1024):
    t, d = x.shape
    n = w.shape[1]
    tm = min(tm, seq)
    tn = PROJ_TN
    assert t % tm == 0 and seq % tm == 0 and n % tn == 0
    spt = seq // tm
    cos, sina, sinb = _rope_tables(seq)
    tab_spec = pl.BlockSpec((tm, LANES), lambda i, j: (i % spt, 0))
    plans, cast_w, cast_specs, cast_shapes = _cast_jobs(cast, t // tm, n // tn)
    outs = pl.pallas_call(
        functools.partial(_proj_kernel, ncast=len(cast_w)),
        grid=(t // tm, n // tn),
        in_specs=[
            pl.BlockSpec(memory_space=pl.ANY),
            pl.BlockSpec((1, d), lambda i, j: (0, 0)),
            pl.BlockSpec((d, tn), lambda i, j: (0, j)),
            tab_spec, tab_spec, tab_spec,
        ] + cast_specs,
        out_specs=[
            pl.BlockSpec((tm, tn), lambda i, j: (i, j)),
            pl.BlockSpec((tm, tn), lambda i, j: (i, 0)),
        ] + cast_specs,
        out_shape=[
            jax.ShapeDtypeStruct((t, n), BF16),
            jax.ShapeDtypeStruct((t, tn), F32),
        ] + cast_shapes,
        scratch_shapes=[pltpu.VMEM((tm, d), BF16), pltpu.VMEM((tm, d), F32),
                        pltpu.SemaphoreType.DMA(())],
        compiler_params=_params(("arbitrary", "arbitrary")),
        name="proj",
    )(x, norm_w.reshape(1, d), w, cos, sina, sinb, *cast_w)
    return outs[0], outs[1], _merge_cast_results(cast, plans, outs[2:])


def _attn_kernel(q_ref, k_ref, v_ref, lq1_ref, lk1_ref, lq2_ref, lk2_ref, sw_ref, o_ref,
                 vx_ref, *, tq, lambda_init):
    hd = ATTN_HEAD_DIM
    w = HEAD_W
    hq = tq // 2
    seq = q_ref.shape[0]
    nt = (((1,), (1,)), ((), ()))
    vx_ref[:, 0:w] = v_ref[...]
    vx_ref[:, w:2 * w] = jnp.ones(v_ref.shape, BF16)
    lam = (jnp.exp(jnp.sum(lq1_ref[...] * lk1_ref[...], axis=-1, keepdims=True))
           - jnp.exp(jnp.sum(lq2_ref[...] * lk2_ref[...], axis=-1, keepdims=True))
           + lambda_init)
    lane = lax.broadcasted_iota(jnp.int32, (hq, w), 1)
    row = lax.broadcasted_iota(jnp.int32, (hq, hq), 0)
    col = lax.broadcasted_iota(jnp.int32, (hq, hq), 1)
    tri = row >= col
    tri2 = jnp.concatenate([tri, tri], axis=0)
    tri2_all = jnp.concatenate([tri2, jnp.ones((2 * hq, hq), jnp.bool_)], axis=0)

    def update(m, acc, s, vx):
        m_new = jnp.maximum(m, jnp.max(s, axis=-1, keepdims=True))
        alpha = jnp.exp2(m - m_new)
        p = jnp.exp2(s - jnp.concatenate([m_new] * (s.shape[1] // w), axis=1))
        acc = (jnp.concatenate([alpha, alpha], axis=1) * acc
               + jnp.dot(p.astype(BF16), vx, preferred_element_type=F32))
        return m_new, acc

    for qi in range(seq // tq):
        parts = []
        for r0 in (qi * tq, qi * tq + hq):
            q = q_ref[r0:r0 + hq, :]
            zero = jnp.zeros_like(q)
            parts += [jnp.where(lane < hd, q, zero), jnp.where(lane >= hd, q, zero)]
        qs = jnp.concatenate(parts, axis=0)
        m = jnp.full((2 * tq, w), -jnp.inf, F32)
        acc = jnp.zeros((2 * tq, 2 * w), F32)
        for kb in range(qi):
            keys = slice(kb * tq, (kb + 1) * tq)
            s = lax.dot_general(qs, k_ref[keys, :], nt, preferred_element_type=F32)
            m, acc = update(m, acc, s, vx_ref[keys, :])
        keys = slice(qi * tq, qi * tq + hq)
        s = lax.dot_general(qs, k_ref[keys, :], nt, preferred_element_type=F32)
        m, acc = update(m, acc, jnp.where(tri2_all, s, -jnp.inf), vx_ref[keys, :])
        keys = slice(qi * tq + hq, (qi + 1) * tq)
        s = lax.dot_general(qs[tq:, :], k_ref[keys, :], nt, preferred_element_type=F32)
        m_hi, acc_hi = update(m[tq:, :], acc[tq:, :], jnp.where(tri2, s, -jnp.inf), vx_ref[keys, :])
        acc = jnp.concatenate([acc[:tq, :], acc_hi], axis=0)
        o = acc[:, 0:w] / acc[:, w:2 * w]
        o = jnp.concatenate([o[0:hq, :] - lam * o[hq:tq, :],
                             o[tq:tq + hq, :] - lam * o[tq + hq:2 * tq, :]], axis=0)
        o_ref[qi * tq:(qi + 1) * tq, :] = (_rms(o, sw_ref[...]) * (1.0 - lambda_init)).astype(BF16)


def _attn(proj, lq1, lk1, lq2, lk2, subln_w, batch, seq, *, layer=0, tq=512):
    t = proj.shape[0]
    tq = min(tq, seq)
    assert seq % tq == 0 and tq % HEAD_W == 0
    heads = ATTN_HEADS
    gpt = PROJ_TN // HEAD_W
    lambda_init = 0.8 - 0.6 * math.exp(-0.3 * layer)
    vec = lambda a: a.reshape(1, -1).astype(F32)
    small = lambda w: pl.BlockSpec((1, w), lambda b, h: (0, 0))
    blk = lambda col: pl.BlockSpec((seq, HEAD_W), lambda b, h: (b, col * gpt + h))
    return pl.pallas_call(
        functools.partial(_attn_kernel, tq=tq, lambda_init=lambda_init),
        grid=(batch, heads),
        in_specs=[
            blk(COL_QA), blk(COL_KA), blk(COL_VA),
            small(ATTN_HEAD_DIM), small(ATTN_HEAD_DIM), small(ATTN_HEAD_DIM), small(ATTN_HEAD_DIM),
            small(HEAD_W),
        ],
        out_specs=pl.BlockSpec((seq, HEAD_W), lambda b, h: (b, h)),
        out_shape=jax.ShapeDtypeStruct((t, heads * HEAD_W), BF16),
        scratch_shapes=[pltpu.VMEM((seq, 2 * HEAD_W), BF16)],
        compiler_params=_params(("parallel", "parallel")),
        name="attn",
    )(proj, proj, proj, vec(lq1), vec(lk1), vec(lq2), vec(lk2), vec(subln_w))


def _hgrn2_kernel(q_ref, fl_ref, i_ref, g_ref, lbraw_ref, gw_ref, o_ref,
                  g_s, kf_s, qd_s, intra_s, upd_s, eg_s, st_s, *, nchunks, layer):
    c = REC_CHUNK
    nt = (((1,), (1,)), ((), ()))
    tn = (((0,), (0,)), ((), ()))
    raw = lbraw_ref[...]
    e = jnp.exp(raw - jnp.max(raw, axis=0, keepdims=True))
    lb = jnp.sum(e[0:layer + 1], axis=0, keepdims=True) / jnp.sum(e, axis=0, keepdims=True)
    gw = gw_ref[...]
    seq = nchunks * c
    grp = REC_GROUP
    gr = grp * c
    ngroups = nchunks // grp

    def sigmoid(x):
        return 0.5 * jnp.tanh(0.5 * x) + 0.5

    f = lb + (1.0 - lb) * sigmoid(fl_ref[...])
    kf_s[...] = 1.0 - f
    g = jnp.log2(f)
    pos = lax.broadcasted_iota(jnp.int32, (seq, HEAD_W), 0) % c
    shift = 1
    while shift < c:
        g = g + jnp.where(pos >= shift, pltpu.roll(g, shift, 0), 0.0)
        shift *= 2
    g_s[...] = g
    factorable = jnp.min(g) >= -REC_SAFE_SPAN_LOG2

    def chunk_last(gcum):
        n = gcum.shape[0] // c
        return jnp.concatenate(
            [jnp.broadcast_to(gcum[(j + 1) * c - 1:(j + 1) * c, :], (c, HEAD_W)) for j in range(n)],
            axis=0)

    for gi in range(ngroups):
        rows = slice(gi * gr, (gi + 1) * gr)
        q = q_ref[rows, :].astype(F32)
        gcum = g_s[rows, :]
        qd_s[rows, :] = (q * sigmoid(q) * jnp.exp2(gcum)).astype(BF16)
        glast = chunk_last(gcum)
        kd = (kf_s[rows, :] * jnp.exp2(glast - gcum)).astype(BF16)
        v = i_ref[rows, :]
        for j in range(grp):
            ci = gi * grp + j
            sub = slice(j * c, (j + 1) * c)
            upd_s[ci] = lax.dot_general(v[sub, :], kd[sub, :], tn, preferred_element_type=F32)
            eg_s[ci] = jnp.exp2(glast[j * c:j * c + SUBLANES, :])

    @pl.when(factorable)
    def _():
        row = lax.broadcasted_iota(jnp.int32, (gr, gr), 0)
        col = lax.broadcasted_iota(jnp.int32, (gr, gr), 1)
        keep = (row >= col) & ((row // c) == (col // c))
        for gi in range(ngroups):
            rows = slice(gi * gr, (gi + 1) * gr)
            ku = (kf_s[rows, :] * jnp.exp2(-g_s[rows, :])).astype(BF16)
            a = lax.dot_general(qd_s[rows, :], ku, nt, preferred_element_type=F32)
            a = jnp.where(keep, a, 0.0).astype(BF16)
            intra_s[rows, :] = jnp.dot(a, i_ref[rows, :], preferred_element_type=F32)

    @pl.when(jnp.logical_not(factorable))
    def _():
        row = lax.broadcasted_iota(jnp.int32, (c, c), 0)
        col = lax.broadcasted_iota(jnp.int32, (c, c), 1)

        def chunk(ci, _):
            rows = pl.ds(pl.multiple_of(ci * c, c), c)
            q = q_ref[rows, :].astype(F32)
            qf = q * sigmoid(q)
            gcum = g_s[rows, :]

            def column(s, a):
                src = pl.ds(ci * c + s, 1)
                term = qf * kf_s[src, :] * jnp.exp2(jnp.minimum(gcum - g_s[src, :], 0.0))
                return jnp.where(col == s, jnp.sum(term, axis=1, keepdims=True), a)

            a = lax.fori_loop(0, c, column, jnp.zeros((c, c), F32))
            a = jnp.where(row >= col, a, 0.0).astype(BF16)
            intra_s[rows, :] = jnp.dot(a, i_ref[rows, :], preferred_element_type=F32)
            return 0

        lax.fori_loop(0, nchunks, chunk, 0)

    st = jnp.zeros((HEAD_W, HEAD_W), F32)
    for ci in range(nchunks):
        st_s[ci] = st.astype(BF16)
        st = st * jnp.broadcast_to(eg_s[ci][0:1, :], st.shape) + upd_s[ci]

    for gi in range(ngroups):
        rows = slice(gi * gr, (gi + 1) * gr)
        inter = jnp.concatenate(
            [lax.dot_general(qd_s[(gi * grp + j) * c:(gi * grp + j + 1) * c, :], st_s[gi * grp + j], nt,
                             preferred_element_type=F32) for j in range(grp)], axis=0)
        o = _rms(inter + intra_s[rows, :], gw)
        gate = g_ref[rows, :].astype(F32)
        o_ref[rows, :] = (o * (gate * sigmoid(gate))).astype(BF16)


def _hgrn2(proj, fl, lb_raw, gnorm_w, batch, seq, *, layer=0):
    t = proj.shape[0]
    heads = fl.shape[1] // HEAD_W
    gpt = PROJ_TN // HEAD_W
    depth1 = lb_raw.shape[0]
    assert seq % (REC_CHUNK * REC_GROUP) == 0
    nchunks = seq // REC_CHUNK
    blk = lambda col: pl.BlockSpec((seq, HEAD_W), lambda b, h: (b, col * gpt + h))
    return pl.pallas_call(
        functools.partial(_hgrn2_kernel, nchunks=nchunks, layer=layer),
        grid=(batch, heads),
        in_specs=[
            blk(COL_QR),
            pl.BlockSpec((seq, HEAD_W), lambda b, h: (b, h)),
            blk(COL_IR),
            blk(COL_GR),
            pl.BlockSpec((depth1, HEAD_W), lambda b, h: (0, h)),
            pl.BlockSpec((1, HEAD_W), lambda b, h: (0, 0)),
        ],
        out_specs=pl.BlockSpec((seq, HEAD_W), lambda b, h: (b, h)),
        out_shape=jax.ShapeDtypeStruct((t, heads * HEAD_W), BF16),
        scratch_shapes=[
            pltpu.VMEM((seq, HEAD_W), F32),
            pltpu.VMEM((seq, HEAD_W), F32),
            pltpu.VMEM((seq, HEAD_W), BF16),
            pltpu.VMEM((seq, HEAD_W), F32),
            pltpu.VMEM((nchunks, HEAD_W, HEAD_W), F32),
            pltpu.VMEM((nchunks, SUBLANES, HEAD_W), F32),
            pltpu.VMEM((nchunks, HEAD_W, HEAD_W), BF16),
        ],
        compiler_params=_params(("parallel", "parallel")),
        name="hgrn2",
    )(proj, fl, proj, proj, lb_raw.astype(F32), gnorm_w.reshape(1, -1).astype(F32))


def _merge_kernel(x_ref, ya_ref, yr_ref, ga0_ref, ga1_ref, gb0_ref, gb1_ref, wa_ref, wr_ref, wo_ref,
                  o_ref):
    pa = jnp.dot(ya_ref[...], wa_ref[...], preferred_element_type=F32)
    pr = jnp.dot(yr_ref[...], wr_ref[...], preferred_element_type=F32)
    ga = jnp.concatenate([ga0_ref[...], ga1_ref[...]], axis=1).astype(F32)
    gb = jnp.concatenate([gb0_ref[...], gb1_ref[...]], axis=1).astype(F32)
    merged = jax.nn.sigmoid(ga) * pa + jax.nn.sigmoid(gb) * pr
    o_ref[...] = x_ref[...] + jnp.dot(merged.astype(BF16), wo_ref[...], preferred_element_type=F32)


def _merge(x, ya, yr, proj, wa, wr, wo, *, tm=512):
    t, d = x.shape
    tm = min(tm, t)
    wa_k, wr_k = wa.shape[0], wr.shape[0]
    assert d == 2 * PROJ_TN
    const = lambda shape: pl.BlockSpec(shape, lambda i: (0, 0), pipeline_mode=pl.Buffered(1))
    gate = lambda col: pl.BlockSpec((tm, PROJ_TN), lambda i: (i, col))
    return pl.pallas_call(
        _merge_kernel,
        grid=(t // tm,),
        in_specs=[
            pl.BlockSpec((tm, d), lambda i: (i, 0)),
            pl.BlockSpec((tm, wa_k), lambda i: (i, 0)),
            pl.BlockSpec((tm, wr_k), lambda i: (i, 0)),
            gate(COL_GA), gate(COL_GA + 1), gate(COL_GB), gate(COL_GB + 1),
            const((wa_k, d)), const((wr_k, d)), const((d, d)),
        ],
        out_specs=pl.BlockSpec((tm, d), lambda i: (i, 0)),
        out_shape=jax.ShapeDtypeStruct((t, d), F32),
        compiler_params=_params(("parallel",)),
        name="merge",
    )(x, ya, yr, proj, proj, proj, proj, wa, wr, wo)


def kernel(x, ffn1_norm, ffn1_in, ffn1_out, mix_norm, w_in, lambda_q1, lambda_k1, lambda_q2,
           lambda_k2, attn_subln, rec_lb_raw, rec_gnorm, w_proj_attn, w_proj_rec, w_out,
           ffn2_norm, ffn2_in, ffn2_out, final_norm):
    batch, seq, d = x.shape
    depth = ffn1_in.shape[0]
    bf = lambda w: w.astype(BF16)
    xt = x.reshape(batch * seq, d)
    for l in range(depth):
        tm = min(FFN_TM, batch * seq)
        if batch * seq >= 2 * tm:
            head, f1_gate_b, f1_up_b, f1_out_b = _ffn_head(xt, ffn1_norm[l], ffn1_in[l], ffn1_out[l], tm=tm)
        else:
            head, f1_gate_b, f1_up_b, f1_out_b = None, bf(ffn1_in[l]), bf(ffn1_in[l]), bf(ffn1_out[l])
        xt, (w_in_b, f2_in_b, f2_out_b) = _ffn(
            xt, ffn1_norm[l], f1_gate_b, f1_up_b, f1_out_b, final_norm, final_norm=False, head=head,
            cast=(w_in[l], ffn2_in[l], ffn2_out[l]), tm=tm)
        proj, fl, (wa_b, wr_b, wo_b) = _proj(xt, mix_norm[l], w_in_b, seq,
                                             cast=(w_proj_attn[l], w_proj_rec[l], w_out[l]))
        ya = _attn(proj, lambda_q1[l], lambda_k1[l], lambda_q2[l], lambda_k2[l], attn_subln[l],
                   batch, seq, layer=l)
        yr = _hgrn2(proj, fl, rec_lb_raw, rec_gnorm[l], batch, seq, layer=l)
        xt = _merge(xt, ya, yr, proj, wa_b, wr_b, wo_b)
        xt, _ = _ffn(xt, ffn2_norm[l], f2_in_b, f2_in_b, f2_out_b, final_norm,
                     final_norm=(l == depth - 1), tm=tm)
    return xt.reshape(batch, seq, d)
```

```python
import functools
import math

import jax
import jax.numpy as jnp
from jax import lax
from jax.experimental import pallas as pl
from jax.experimental.pallas import tpu as pltpu

F32 = jnp.float32
BF16 = jnp.bfloat16

EPS = 1e-6
ATTN_HEADS = 8
ATTN_HEAD_DIM = 64
ROPE_THETA = 500000.0
ROT_DIM = ATTN_HEAD_DIM // 4
HEAD_W = 128
REC_CHUNK = 64
REC_GROUP = 4
REC_SAFE_SPAN_LOG2 = 115.0
LANES = 128
SUBLANES = 8
BF16_ROWS = 16
FFN_TM = 1024
PROJ_TN = 1024
COL_QA, COL_KA, COL_VA, COL_QR, COL_FR, COL_IR, COL_GR, COL_GA, COL_GB = 0, 1, 2, 3, 4, 5, 6, 7, 9
VMEM_LIMIT = 52 * 1024 * 1024
PROJ_VMEM_LIMIT = 61 * 1024 * 1024
CAST_BLOCK_BYTES = 1024 * 1024


def _rms(x, w):
    return x * lax.rsqrt(jnp.mean(x * x, axis=-1, keepdims=True) + EPS) * w


def _params(sem, vmem_limit=VMEM_LIMIT):
    return pltpu.CompilerParams(dimension_semantics=sem, vmem_limit_bytes=vmem_limit)


def _row_tile_prefetch(x_hbm, xbuf, sem, consume, first=0):
    i, s = pl.program_id(0), pl.program_id(1)
    tm = xbuf.shape[0]

    def copy(tile):
        return pltpu.make_async_copy(x_hbm.at[pl.ds(pl.multiple_of(tile * tm, tm), tm), :], xbuf, sem)

    if first == 0:
        @pl.when((s == 0) & (i == 0))
        def _():
            copy(0).start()

    @pl.when((s == 0) & (i >= first))
    def _():
        copy(i).wait()
        consume(xbuf[...])

    @pl.when((s == 1) & (i + 1 >= first) & (i + 1 < pl.num_programs(0)))
    def _():
        copy(i + 1).start()


def _cast_plan(shape, ni, ninner):
    r, c = shape
    small = lambda rows, cols: rows * cols * 4 <= CAST_BLOCK_BYTES
    for nsteps, step in ((ni * ninner, lambda i, s: i * ninner + s), (ni, lambda i, s: i)):
        if r % nsteps == 0 and (r // nsteps) % BF16_ROWS == 0 and small(r // nsteps, c):
            return (r // nsteps, c), (lambda i, s, step=step: (step(i, s), 0))
        ncol = c // LANES
        if c % LANES == 0 and nsteps % ncol == 0:
            split = nsteps // ncol
            if r % split == 0 and (r // split) % BF16_ROWS == 0 and small(r // split, LANES):
                return ((r // split, LANES),
                        (lambda i, s, step=step, split=split: (step(i, s) % split, step(i, s) // split)))
    return None


def _cast_jobs(weights, ni, ninner):
    plans = [_cast_plan(w.shape, ni, ninner) for w in weights]
    ride = [(w, p) for w, p in zip(weights, plans) if p is not None]
    in_specs = [pl.BlockSpec(p[0], p[1]) for _, p in ride]
    out_shapes = [jax.ShapeDtypeStruct(w.shape, BF16) for w, _ in ride]
    return plans, [w for w, _ in ride], in_specs, out_shapes


def _merge_cast_results(weights, plans, cast_outs):
    outs = iter(cast_outs)
    return [next(outs) if p is not None else w.astype(BF16) for w, p in zip(weights, plans)]


def _swiglu_half(h, wg, wu, wo):
    g = jnp.dot(h, wg, preferred_element_type=F32)
    u = jnp.dot(h, wu, preferred_element_type=F32)
    a = (g * jax.nn.sigmoid(g) * u * 0.5).astype(BF16)
    return jnp.dot(a, wo, preferred_element_type=F32)


def _ffn_head_kernel(x_ref, nw_ref, wg_ref, wu_ref, wo_ref, o_ref, wg_b, wu_b, wo_b, h_ref):
    @pl.when(pl.program_id(0) == 0)
    def _():
        x = x_ref[...]
        h_ref[...] = _rms(x, nw_ref[...]).astype(BF16)
        o_ref[...] = x

    wg_b[...] = wg_ref[...].astype(BF16)
    wu_b[...] = wu_ref[...].astype(BF16)
    wo_b[...] = wo_ref[...].astype(BF16)
    o_ref[...] += _swiglu_half(h_ref[...], wg_b[...], wu_b[...], wo_b[...])


def _ffn_head(x, norm_w, w_in, w_out, *, tm, tf=256):
    t, d = x.shape
    ff = w_out.shape[0]
    nf = ff // tf
    assert ff % tf == 0 and t % tm == 0
    return pl.pallas_call(
        _ffn_head_kernel,
        grid=(nf,),
        in_specs=[
            pl.BlockSpec((tm, d), lambda f: (0, 0), pipeline_mode=pl.Buffered(1)),
            pl.BlockSpec((1, d), lambda f: (0, 0)),
            pl.BlockSpec((d, tf), lambda f: (0, f)),
            pl.BlockSpec((d, tf), lambda f: (0, f + nf)),
            pl.BlockSpec((tf, d), lambda f: (f, 0)),
        ],
        out_specs=[
            pl.BlockSpec((tm, d), lambda f: (0, 0)),
            pl.BlockSpec((d, tf), lambda f: (0, f)),
            pl.BlockSpec((d, tf), lambda f: (0, f)),
            pl.BlockSpec((tf, d), lambda f: (f, 0)),
        ],
        out_shape=[
            jax.ShapeDtypeStruct((tm, d), F32),
            jax.ShapeDtypeStruct((d, ff), BF16),
            jax.ShapeDtypeStruct((d, ff), BF16),
            jax.ShapeDtypeStruct((ff, d), BF16),
        ],
        scratch_shapes=[pltpu.VMEM((tm, d), BF16)],
        compiler_params=_params(("arbitrary",)),
        name="ffn_head",
    )(x, norm_w.reshape(1, d), w_in, w_in, w_out)


def _ffn_kernel(x_hbm, nw_ref, wg_ref, wu_ref, wo_ref, fw_ref, *rest, nf, ncast, has_head,
                final_norm):
    head_hbm = rest[0] if has_head else None
    rest = rest[has_head:]
    cast_in = rest[:ncast]
    o_ref = rest[ncast]
    cast_out = rest[ncast + 1:2 * ncast + 1]
    h_ref, xbuf, sem = rest[2 * ncast + 1:]
    i, f = pl.program_id(0), pl.program_id(1)

    def side_casts():
        for src, dst in zip(cast_in, cast_out):
            dst[...] = src[...].astype(BF16)

    def first_step(x):
        side_casts()
        h = _rms(x, nw_ref[...]).astype(BF16)
        h_ref[...] = h
        o_ref[...] = x + _swiglu_half(h, wg_ref[...], wu_ref[...], wo_ref[...])

    _row_tile_prefetch(x_hbm, xbuf, sem, first_step, first=int(has_head))

    last = nf - 1 if final_norm else nf

    @pl.when((f > 0) & (f < last) & (i >= int(has_head)))
    def _():
        side_casts()
        o_ref[...] += _swiglu_half(h_ref[...], wg_ref[...], wu_ref[...], wo_ref[...])

    if has_head:
        @pl.when(i == 0)
        def _():
            side_casts()

        @pl.when((i == 0) & (f == 0))
        def _():
            copy = pltpu.make_async_copy(head_hbm, o_ref, sem)
            copy.start()
            copy.wait()

    if final_norm:
        @pl.when(f == nf - 1)
        def _():
            side_casts()
            o = o_ref[...] + _swiglu_half(h_ref[...], wg_ref[...], wu_ref[...], wo_ref[...])
            o_ref[...] = _rms(o, fw_ref[...])


def _ffn(x, norm_w, w_gate, w_up, w_out, final_w, *, final_norm, head=None, cast=(), tm=FFN_TM, tf=512):
    t, d = x.shape
    ff = w_out.shape[0]
    tm, tf = min(tm, t), min(tf, ff)
    nf = ff // tf
    assert t % tm == 0 and ff % tf == 0 and nf >= 2
    assert head is None or (head.shape == (tm, d) and t // tm >= 2 and not final_norm)
    up0 = nf if w_up.shape[1] == 2 * ff else 0
    hold = (lambda i, f: f) if head is None else (lambda i, f: jnp.where(i == 0, 0, f))
    plans, cast_w, cast_specs, cast_shapes = _cast_jobs(cast, t // tm, nf)
    outs = pl.pallas_call(
        functools.partial(_ffn_kernel, nf=nf, ncast=len(cast_w), has_head=head is not None,
                          final_norm=final_norm),
        grid=(t // tm, nf),
        in_specs=[
            pl.BlockSpec(memory_space=pl.ANY),
            pl.BlockSpec((1, d), lambda i, f: (0, 0)),
            pl.BlockSpec((d, tf), lambda i, f: (0, hold(i, f))),
            pl.BlockSpec((d, tf), lambda i, f: (0, hold(i, f) + up0)),
            pl.BlockSpec((tf, d), lambda i, f: (hold(i, f), 0)),
            pl.BlockSpec((1, d), lambda i, f: (0, 0)),
        ] + ([pl.BlockSpec(memory_space=pl.ANY)] if head is not None else []) + cast_specs,
        out_specs=[pl.BlockSpec((tm, d), lambda i, f: (i, 0))] + cast_specs,
        out_shape=[jax.ShapeDtypeStruct((t, d), F32)] + cast_shapes,
        scratch_shapes=[pltpu.VMEM((tm, d), BF16), pltpu.VMEM((tm, d), F32),
                        pltpu.SemaphoreType.DMA(())],
        compiler_params=_params(("arbitrary", "arbitrary")),
        name="ffn",
    )(x, norm_w.reshape(1, d), w_gate, w_up, w_out, final_w.reshape(1, d),
      *([head] if head is not None else []), *cast_w)
    return outs[0], _merge_cast_results(cast, plans, outs[1:])


def _proj_kernel(x_hbm, nw_ref, w_ref, cos_ref, sina_ref, sinb_ref, *rest, ncast, nsteps):
    cast_in = rest[:ncast]
    o_ref, fl_ref = rest[ncast:ncast + 2]
    cast_out = rest[ncast + 2:2 * ncast + 2]
    h_ref, xbuf, sem = rest[2 * ncast + 2:]
    j = pl.program_id(1)
    tn = PROJ_TN

    def project(h, width):
        for src, dst in zip(cast_in, cast_out):
            dst[...] = src[...].astype(BF16)
        return jnp.dot(h, w_ref[:, 0:width], preferred_element_type=F32)

    def store_rotated(r, c_base, scale):
        half = ROT_DIM // 2
        cos, sina, sinb = (t[...] if scale == 1.0 else t[...] * scale
                           for t in (cos_ref, sina_ref, sinb_ref))
        for c0 in range(0, r.shape[1], LANES):
            rc = r[:, c0:c0 + LANES]
            rot = rc * cos + pltpu.roll(rc, LANES - half, 1) * sina + pltpu.roll(rc, half, 1) * sinb
            o_ref[:, c_base + c0:c_base + c0 + LANES] = rot.astype(BF16)

    def first_step(x):
        h = _rms(x, nw_ref[...]).astype(BF16)
        h_ref[...] = h
        r = project(h, 2 * tn)
        store_rotated(r[:, :tn], 0, ATTN_HEAD_DIM ** -0.5 * math.log2(math.e))
        store_rotated(r[:, tn:], tn, 1.0)

    assert (COL_QA, COL_KA) == (0, 1) and COL_FR % 2 == 0
    _row_tile_prefetch(x_hbm, xbuf, sem, first_step)

    @pl.when(j == COL_FR // 2)
    def _():
        r = project(h_ref[...], 2 * tn)
        fl_ref[...] = r[:, :tn]
        o_ref[...] = r.astype(BF16)

    @pl.when((j > 0) & (j != COL_FR // 2) & (j < nsteps - 1))
    def _():
        o_ref[...] = project(h_ref[...], 2 * tn).astype(BF16)

    @pl.when(j == nsteps - 1)
    def _():
        o_ref[:, 0:tn] = project(h_ref[...], tn).astype(BF16)


def _rope_tables(seq):
    half = ROT_DIM // 2
    inv_freq = ROPE_THETA ** (-jnp.arange(0, ROT_DIM, 2, dtype=F32) / ROT_DIM)
    ang = jnp.arange(seq).astype(F32)[:, None] * inv_freq[None, :]
    cos8, sin8 = jnp.cos(ang), jnp.sin(ang)
    pad = ATTN_HEAD_DIM - ROT_DIM
    ones = jnp.ones((seq, pad), F32)
    zeros = jnp.zeros((seq, pad), F32)
    z8 = jnp.zeros((seq, half), F32)
    cos = jnp.concatenate([cos8, cos8, ones], axis=1)
    sina = jnp.concatenate([-sin8, z8, zeros], axis=1)
    sinb = jnp.concatenate([z8, sin8, zeros], axis=1)
    rep = LANES // ATTN_HEAD_DIM
    return tuple(jnp.tile(a, (1, rep)) for a in (cos, sina, sinb))


def _proj(x, norm_w, w, seq, *, cast=(), tm=1024):
    t, d = x.shape
    n = w.shape[1]
    tm = min(tm, seq)
    tn = PROJ_TN
    assert t % tm == 0 and seq % tm == 0 and n % tn == 0 and (n // tn) % 2 == 1
    nsteps = (n // tn + 1) // 2
    spt = seq // tm
    cos, sina, sinb = _rope_tables(seq)
    tab_spec = pl.BlockSpec((tm, LANES), lambda i, j: (i % spt, 0))
    plans, cast_w, cast_specs, cast_shapes = _cast_jobs(cast, t // tm, nsteps)
    outs = pl.pallas_call(
        functools.partial(_proj_kernel, ncast=len(cast_w), nsteps=nsteps),
        grid=(t // tm, nsteps),
        in_specs=[
            pl.BlockSpec(memory_space=pl.ANY),
            pl.BlockSpec((1, d), lambda i, j: (0, 0)),
            pl.BlockSpec((d, 2 * tn), lambda i, j: (0, j)),
            tab_spec, tab_spec, tab_spec,
        ] + cast_specs,
        out_specs=[
            pl.BlockSpec((tm, 2 * tn), lambda i, j: (i, j)),
            pl.BlockSpec((tm, tn), lambda i, j: (i, 0)),
        ] + cast_specs,
        out_shape=[
            jax.ShapeDtypeStruct((t, n), BF16),
            jax.ShapeDtypeStruct((t, tn), F32),
        ] + cast_shapes,
        scratch_shapes=[pltpu.VMEM((tm, d), BF16), pltpu.VMEM((tm, d), F32),
                        pltpu.SemaphoreType.DMA(())],
        compiler_params=_params(("arbitrary", "arbitrary"), PROJ_VMEM_LIMIT),
        name="proj",
    )(x, norm_w.reshape(1, d), w, cos, sina, sinb, *cast_w)
    return outs[0], outs[1], _merge_cast_results(cast, plans, outs[2:])


def _attn_kernel(q_ref, k_ref, v_ref, lq1_ref, lk1_ref, lq2_ref, lk2_ref, sw_ref, o_ref,
                 vx_ref, *, tq, lambda_init):
    hd = ATTN_HEAD_DIM
    w = HEAD_W
    hq = tq // 2
    seq = q_ref.shape[0]
    nt = (((1,), (1,)), ((), ()))
    vx_ref[:, 0:w] = v_ref[...]
    vx_ref[:, w:2 * w] = jnp.ones(v_ref.shape, BF16)
    lam = (jnp.exp(jnp.sum(lq1_ref[...] * lk1_ref[...], axis=-1, keepdims=True))
           - jnp.exp(jnp.sum(lq2_ref[...] * lk2_ref[...], axis=-1, keepdims=True))
           + lambda_init)
    lane = lax.broadcasted_iota(jnp.int32, (hq, w), 1)
    row = lax.broadcasted_iota(jnp.int32, (hq, hq), 0)
    col = lax.broadcasted_iota(jnp.int32, (hq, hq), 1)
    tri = row >= col
    tri2 = jnp.concatenate([tri, tri], axis=0)
    tri2_all = jnp.concatenate([tri2, jnp.ones((2 * hq, hq), jnp.bool_)], axis=0)

    def update(m, acc, s, vx):
        m_new = jnp.maximum(m, jnp.max(s, axis=-1, keepdims=True))
        alpha = jnp.exp2(m - m_new)
        p = jnp.exp2(s - jnp.concatenate([m_new] * (s.shape[1] // w), axis=1))
        acc = (jnp.concatenate([alpha, alpha], axis=1) * acc
               + jnp.dot(p.astype(BF16), vx, preferred_element_type=F32))
        return m_new, acc

    for qi in range(seq // tq):
        parts = []
        for r0 in (qi * tq, qi * tq + hq):
            q = q_ref[r0:r0 + hq, :]
            zero = jnp.zeros_like(q)
            parts += [jnp.where(lane < hd, q, zero), jnp.where(lane >= hd, q, zero)]
        qs = jnp.concatenate(parts, axis=0)
        m = jnp.full((2 * tq, w), -jnp.inf, F32)
        acc = jnp.zeros((2 * tq, 2 * w), F32)
        for kb in range(2 * qi):
            keys = slice(kb * hq, (kb + 1) * hq)
            s = lax.dot_general(qs, k_ref[keys, :], nt, preferred_element_type=F32)
            m, acc = update(m, acc, s, vx_ref[keys, :])
        keys = slice(qi * tq, qi * tq + hq)
        s = lax.dot_general(qs, k_ref[keys, :], nt, preferred_element_type=F32)
        m, acc = update(m, acc, jnp.where(tri2_all, s, -jnp.inf), vx_ref[keys, :])
        keys = slice(qi * tq + hq, (qi + 1) * tq)
        s = lax.dot_general(qs[tq:, :], k_ref[keys, :], nt, preferred_element_type=F32)
        m_hi, acc_hi = update(m[tq:, :], acc[tq:, :], jnp.where(tri2, s, -jnp.inf), vx_ref[keys, :])
        acc = jnp.concatenate([acc[:tq, :], acc_hi], axis=0)
        o = acc[:, 0:w] / acc[:, w:2 * w]
        o = jnp.concatenate([o[0:hq, :] - lam * o[hq:tq, :],
                             o[tq:tq + hq, :] - lam * o[tq + hq:2 * tq, :]], axis=0)
        o_ref[qi * tq:(qi + 1) * tq, :] = (_rms(o, sw_ref[...]) * (1.0 - lambda_init)).astype(BF16)


def _attn(proj, lq1, lk1, lq2, lk2, subln_w, batch, seq, *, layer=0, tq=512):
    t = proj.shape[0]
    tq = min(tq, seq)
    assert seq % tq == 0 and tq % HEAD_W == 0
    heads = ATTN_HEADS
    gpt = PROJ_TN // HEAD_W
    lambda_init = 0.8 - 0.6 * math.exp(-0.3 * layer)
    vec = lambda a: a.reshape(1, -1).astype(F32)
    small = lambda w: pl.BlockSpec((1, w), lambda b, h: (0, 0))
    blk = lambda col: pl.BlockSpec((seq, HEAD_W), lambda b, h: (b, col * gpt + h))
    return pl.pallas_call(
        functools.partial(_attn_kernel, tq=tq, lambda_init=lambda_init),
        grid=(batch, heads),
        in_specs=[
            blk(COL_QA), blk(COL_KA), blk(COL_VA),
            small(ATTN_HEAD_DIM), small(ATTN_HEAD_DIM), small(ATTN_HEAD_DIM), small(ATTN_HEAD_DIM),
            small(HEAD_W),
        ],
        out_specs=pl.BlockSpec((seq, HEAD_W), lambda b, h: (b, h)),
        out_shape=jax.ShapeDtypeStruct((t, heads * HEAD_W), BF16),
        scratch_shapes=[pltpu.VMEM((seq, 2 * HEAD_W), BF16)],
        compiler_params=_params(("parallel", "parallel")),
        name="attn",
    )(proj, proj, proj, vec(lq1), vec(lk1), vec(lq2), vec(lk2), vec(subln_w))


def _hgrn2_kernel(q_ref, fl_ref, i_ref, g_ref, lbraw_ref, gw_ref, o_ref,
                  g_s, kf_s, qd_s, intra_s, upd_s, eg_s, st_s, *, nchunks, layer):
    c = REC_CHUNK
    nt = (((1,), (1,)), ((), ()))
    tn = (((0,), (0,)), ((), ()))
    raw = lbraw_ref[...]
    e = jnp.exp(raw - jnp.max(raw, axis=0, keepdims=True))
    lb = jnp.sum(e[0:layer + 1], axis=0, keepdims=True) / jnp.sum(e, axis=0, keepdims=True)
    gw = gw_ref[...]
    seq = nchunks * c
    grp = REC_GROUP
    gr = grp * c
    ngroups = nchunks // grp

    def sigmoid(x):
        return 0.5 * jnp.tanh(0.5 * x) + 0.5

    f = lb + (1.0 - lb) * sigmoid(fl_ref[...])
    kf_s[...] = 1.0 - f
    g = jnp.log2(f)
    pos = lax.broadcasted_iota(jnp.int32, (seq, HEAD_W), 0) % c
    shift = 1
    while shift < c:
        g = g + jnp.where(pos >= shift, pltpu.roll(g, shift, 0), 0.0)
        shift *= 2
    g_s[...] = g
    factorable = jnp.min(g) >= -REC_SAFE_SPAN_LOG2

    def chunk_last(gcum):
        n = gcum.shape[0] // c
        return jnp.concatenate(
            [jnp.broadcast_to(gcum[(j + 1) * c - 1:(j + 1) * c, :], (c, HEAD_W)) for j in range(n)],
            axis=0)

    for gi in range(ngroups):
        rows = slice(gi * gr, (gi + 1) * gr)
        q = q_ref[rows, :].astype(F32)
        gcum = g_s[rows, :]
        qd_s[rows, :] = (q * sigmoid(q) * jnp.exp2(gcum)).astype(BF16)
        glast = chunk_last(gcum)
        kd = (kf_s[rows, :] * jnp.exp2(glast - gcum)).astype(BF16)
        v = i_ref[rows, :]
        for j in range(grp):
            ci = gi * grp + j
            sub = slice(j * c, (j + 1) * c)
            upd_s[ci] = lax.dot_general(v[sub, :], kd[sub, :], tn, preferred_element_type=F32)
            eg_s[ci] = jnp.exp2(glast[j * c:j * c + SUBLANES, :])

    @pl.when(factorable)
    def _():
        row = lax.broadcasted_iota(jnp.int32, (gr, gr), 0)
        col = lax.broadcasted_iota(jnp.int32, (gr, gr), 1)
        keep = (row >= col) & ((row // c) == (col // c))
        for gi in range(ngroups):
            rows = slice(gi * gr, (gi + 1) * gr)
            ku = (kf_s[rows, :] * jnp.exp2(-g_s[rows, :])).astype(BF16)
            a = lax.dot_general(qd_s[rows, :], ku, nt, preferred_element_type=F32)
            a = jnp.where(keep, a, 0.0).astype(BF16)
            intra_s[rows, :] = jnp.dot(a, i_ref[rows, :], preferred_element_type=F32)

    @pl.when(jnp.logical_not(factorable))
    def _():
        row = lax.broadcasted_iota(jnp.int32, (c, c), 0)
        col = lax.broadcasted_iota(jnp.int32, (c, c), 1)

        def chunk(ci, _):
            rows = pl.ds(pl.multiple_of(ci * c, c), c)
            q = q_ref[rows, :].astype(F32)
            qf = q * sigmoid(q)
            gcum = g_s[rows, :]

            def column(s, a):
                src = pl.ds(ci * c + s, 1)
                term = qf * kf_s[src, :] * jnp.exp2(jnp.minimum(gcum - g_s[src, :], 0.0))
                return jnp.where(col == s, jnp.sum(term, axis=1, keepdims=True), a)

            a = lax.fori_loop(0, c, column, jnp.zeros((c, c), F32))
            a = jnp.where(row >= col, a, 0.0).astype(BF16)
            intra_s[rows, :] = jnp.dot(a, i_ref[rows, :], preferred_element_type=F32)
            return 0

        lax.fori_loop(0, nchunks, chunk, 0)

    st = jnp.zeros((HEAD_W, HEAD_W), F32)
    for ci in range(nchunks):
        st_s[ci] = st.astype(BF16)
        st = st * jnp.broadcast_to(eg_s[ci][0:1, :], st.shape) + upd_s[ci]

    for gi in range(ngroups):
        rows = slice(gi * gr, (gi + 1) * gr)
        inter = jnp.concatenate(
            [lax.dot_general(qd_s[(gi * grp + j) * c:(gi * grp + j + 1) * c, :], st_s[gi * grp + j], nt,
                             preferred_element_type=F32) for j in range(grp)], axis=0)
        o = _rms(inter + intra_s[rows, :], gw)
        gate = g_ref[rows, :].astype(F32)
        o_ref[rows, :] = (o * (gate * sigmoid(gate))).astype(BF16)


def _hgrn2(proj, fl, lb_raw, gnorm_w, batch, seq, *, layer=0):
    t = proj.shape[0]
    heads = fl.shape[1] // HEAD_W
    gpt = PROJ_TN // HEAD_W
    depth1 = lb_raw.shape[0]
    assert seq % (REC_CHUNK * REC_GROUP) == 0
    nchunks = seq // REC_CHUNK
    blk = lambda col: pl.BlockSpec((seq, HEAD_W), lambda b, h: (b, col * gpt + h))
    return pl.pallas_call(
        functools.partial(_hgrn2_kernel, nchunks=nchunks, layer=layer),
        grid=(batch, heads),
        in_specs=[
            blk(COL_QR),
            pl.BlockSpec((seq, HEAD_W), lambda b, h: (b, h)),
            blk(COL_IR),
            blk(COL_GR),
            pl.BlockSpec((depth1, HEAD_W), lambda b, h: (0, h)),
            pl.BlockSpec((1, HEAD_W), lambda b, h: (0, 0)),
        ],
        out_specs=pl.BlockSpec((seq, HEAD_W), lambda b, h: (b, h)),
        out_shape=jax.ShapeDtypeStruct((t, heads * HEAD_W), BF16),
        scratch_shapes=[
            pltpu.VMEM((seq, HEAD_W), F32),
            pltpu.VMEM((seq, HEAD_W), F32),
            pltpu.VMEM((seq, HEAD_W), BF16),
            pltpu.VMEM((seq, HEAD_W), F32),
            pltpu.VMEM((nchunks, HEAD_W, HEAD_W), F32),
            pltpu.VMEM((nchunks, SUBLANES, HEAD_W), F32),
            pltpu.VMEM((nchunks, HEAD_W, HEAD_W), BF16),
        ],
        compiler_params=_params(("parallel", "parallel")),
        name="hgrn2",
    )(proj, fl, proj, proj, lb_raw.astype(F32), gnorm_w.reshape(1, -1).astype(F32))


def _merge_kernel(x_ref, ya_ref, yr_ref, ga0_ref, ga1_ref, gb0_ref, gb1_ref, wa_ref, wr_ref, wo_ref,
                  o_ref):
    pa = jnp.dot(ya_ref[...], wa_ref[...], preferred_element_type=F32)
    pr = jnp.dot(yr_ref[...], wr_ref[...], preferred_element_type=F32)
    ga = jnp.concatenate([ga0_ref[...], ga1_ref[...]], axis=1).astype(F32)
    gb = jnp.concatenate([gb0_ref[...], gb1_ref[...]], axis=1).astype(F32)
    merged = jax.nn.sigmoid(ga) * pa + jax.nn.sigmoid(gb) * pr
    o_ref[...] = x_ref[...] + jnp.dot(merged.astype(BF16), wo_ref[...], preferred_element_type=F32)


def _merge(x, ya, yr, proj, wa, wr, wo, *, tm=512):
    t, d = x.shape
    tm = min(tm, t)
    wa_k, wr_k = wa.shape[0], wr.shape[0]
    assert d == 2 * PROJ_TN
    const = lambda shape: pl.BlockSpec(shape, lambda i: (0, 0), pipeline_mode=pl.Buffered(1))
    gate = lambda col: pl.BlockSpec((tm, PROJ_TN), lambda i: (i, col))
    return pl.pallas_call(
        _merge_kernel,
        grid=(t // tm,),
        in_specs=[
            pl.BlockSpec((tm, d), lambda i: (i, 0)),
            pl.BlockSpec((tm, wa_k), lambda i: (i, 0)),
            pl.BlockSpec((tm, wr_k), lambda i: (i, 0)),
            gate(COL_GA), gate(COL_GA + 1), gate(COL_GB), gate(COL_GB + 1),
            const((wa_k, d)), const((wr_k, d)), const((d, d)),
        ],
        out_specs=pl.BlockSpec((tm, d), lambda i: (i, 0)),
        out_shape=jax.ShapeDtypeStruct((t, d), F32),
        compiler_params=_params(("parallel",)),
        name="merge",
    )(x, ya, yr, proj, proj, proj, proj, wa, wr, wo)


def kernel(x, ffn1_norm, ffn1_in, ffn1_out, mix_norm, w_in, lambda_q1, lambda_k1, lambda_q2,
           lambda_k2, attn_subln, rec_lb_raw, rec_gnorm, w_proj_attn, w_proj_rec, w_out,
           ffn2_norm, ffn2_in, ffn2_out, final_norm):
    batch, seq, d = x.shape
    depth = ffn1_in.shape[0]
    bf = lambda w: w.astype(BF16)
    xt = x.reshape(batch * seq, d)
    for l in range(depth):
        tm = min(FFN_TM, batch * seq)
        if batch * seq >= 2 * tm:
            head, f1_gate_b, f1_up_b, f1_out_b = _ffn_head(xt, ffn1_norm[l], ffn1_in[l], ffn1_out[l], tm=tm)
        else:
            head, f1_gate_b, f1_up_b, f1_out_b = None, bf(ffn1_in[l]), bf(ffn1_in[l]), bf(ffn1_out[l])
        xt, (w_in_b, f2_in_b, f2_out_b) = _ffn(
            xt, ffn1_norm[l], f1_gate_b, f1_up_b, f1_out_b, final_norm, final_norm=False, head=head,
            cast=(w_in[l], ffn2_in[l], ffn2_out[l]), tm=tm)
        proj, fl, (wa_b, wr_b, wo_b) = _proj(xt, mix_norm[l], w_in_b, seq,
                                             cast=(w_proj_attn[l], w_proj_rec[l], w_out[l]))
        ya = _attn(proj, lambda_q1[l], lambda_k1[l], lambda_q2[l], lambda_k2[l], attn_subln[l],
                   batch, seq, layer=l)
        yr = _hgrn2(proj, fl, rec_lb_raw, rec_gnorm[l], batch, seq, layer=l)
        xt = _merge(xt, ya, yr, proj, wa_b, wr_b, wo_b)
        xt, _ = _ffn(xt, ffn2_norm[l], f2_in_b, f2_in_b, f2_out_b, final_norm,
                     final_norm=(l == depth - 1), tm=tm)
    return xt.reshape(batch, seq, d)
```

```python
import functools
import math

import jax
import jax.numpy as jnp
from jax import lax
from jax.experimental import pallas as pl
from jax.experimental.pallas import tpu as pltpu

F32 = jnp.float32
BF16 = jnp.bfloat16

EPS = 1e-6
ATTN_HEADS = 8
ATTN_HEAD_DIM = 64
ROPE_THETA = 500000.0
ROT_DIM = ATTN_HEAD_DIM // 4
HEAD_W = 128
REC_CHUNK = 64
REC_GROUP = 4
REC_SAFE_SPAN_LOG2 = 115.0
LANES = 128
SUBLANES = 8
BF16_ROWS = 16
FFN_TM = 1024
PROJ_TN = 1024
COL_QA, COL_KA, COL_VA, COL_QR, COL_FR, COL_IR, COL_GR, COL_GA, COL_GB = 0, 1, 2, 3, 4, 5, 6, 7, 9
VMEM_LIMIT = 52 * 1024 * 1024
PROJ_VMEM_LIMIT = 61 * 1024 * 1024
CAST_BLOCK_BYTES = 1024 * 1024


def _rms(x, w):
    return x * lax.rsqrt(jnp.mean(x * x, axis=-1, keepdims=True) + EPS) * w


def _params(sem, vmem_limit=VMEM_LIMIT):
    return pltpu.CompilerParams(dimension_semantics=sem, vmem_limit_bytes=vmem_limit)


def _row_tile_prefetch(x_hbm, xbuf, sem, consume, first=0):
    i, s = pl.program_id(0), pl.program_id(1)
    tm = xbuf.shape[0]

    def copy(tile):
        return pltpu.make_async_copy(x_hbm.at[pl.ds(pl.multiple_of(tile * tm, tm), tm), :], xbuf, sem)

    if first == 0:
        @pl.when((s == 0) & (i == 0))
        def _():
            copy(0).start()

    @pl.when((s == 0) & (i >= first))
    def _():
        copy(i).wait()
        consume(xbuf[...])

    @pl.when((s == 1) & (i + 1 >= first) & (i + 1 < pl.num_programs(0)))
    def _():
        copy(i + 1).start()


def _cast_plan(shape, ni, ninner):
    r, c = shape
    small = lambda rows, cols: rows * cols * 4 <= CAST_BLOCK_BYTES
    for nsteps, step in ((ni * ninner, lambda i, s: i * ninner + s), (ni, lambda i, s: i)):
        if r % nsteps == 0 and (r // nsteps) % BF16_ROWS == 0 and small(r // nsteps, c):
            return (r // nsteps, c), (lambda i, s, step=step: (step(i, s), 0))
        ncol = c // LANES
        if c % LANES == 0 and nsteps % ncol == 0:
            split = nsteps // ncol
            if r % split == 0 and (r // split) % BF16_ROWS == 0 and small(r // split, LANES):
                return ((r // split, LANES),
                        (lambda i, s, step=step, split=split: (step(i, s) % split, step(i, s) // split)))
    return None


def _cast_jobs(weights, ni, ninner):
    plans = [_cast_plan(w.shape, ni, ninner) for w in weights]
    ride = [(w, p) for w, p in zip(weights, plans) if p is not None]
    in_specs = [pl.BlockSpec(p[0], p[1]) for _, p in ride]
    out_shapes = [jax.ShapeDtypeStruct(w.shape, BF16) for w, _ in ride]
    return plans, [w for w, _ in ride], in_specs, out_shapes


def _merge_cast_results(weights, plans, cast_outs):
    outs = iter(cast_outs)
    return [next(outs) if p is not None else w.astype(BF16) for w, p in zip(weights, plans)]


def _swiglu_half(h, wg, wu, wo):
    g = jnp.dot(h, wg, preferred_element_type=F32)
    u = jnp.dot(h, wu, preferred_element_type=F32)
    a = (g * jax.nn.sigmoid(g) * u * 0.5).astype(BF16)
    return jnp.dot(a, wo, preferred_element_type=F32)


def _ffn_head_kernel(x_ref, nw_ref, wg_ref, wu_ref, wo_ref, o_ref, wg_b, wu_b, wo_b, h_ref):
    @pl.when(pl.program_id(0) == 0)
    def _():
        x = x_ref[...]
        h_ref[...] = _rms(x, nw_ref[...]).astype(BF16)
        o_ref[...] = x

    wg_b[...] = wg_ref[...].astype(BF16)
    wu_b[...] = wu_ref[...].astype(BF16)
    wo_b[...] = wo_ref[...].astype(BF16)
    o_ref[...] += _swiglu_half(h_ref[...], wg_b[...], wu_b[...], wo_b[...])


def _ffn_head(x, norm_w, w_in, w_out, *, tm, tf=256):
    t, d = x.shape
    ff = w_out.shape[0]
    nf = ff // tf
    assert ff % tf == 0 and t % tm == 0
    return pl.pallas_call(
        _ffn_head_kernel,
        grid=(nf,),
        in_specs=[
            pl.BlockSpec((tm, d), lambda f: (0, 0), pipeline_mode=pl.Buffered(1)),
            pl.BlockSpec((1, d), lambda f: (0, 0)),
            pl.BlockSpec((d, tf), lambda f: (0, f)),
            pl.BlockSpec((d, tf), lambda f: (0, f + nf)),
            pl.BlockSpec((tf, d), lambda f: (f, 0)),
        ],
        out_specs=[
            pl.BlockSpec((tm, d), lambda f: (0, 0)),
            pl.BlockSpec((d, tf), lambda f: (0, f)),
            pl.BlockSpec((d, tf), lambda f: (0, f)),
            pl.BlockSpec((tf, d), lambda f: (f, 0)),
        ],
        out_shape=[
            jax.ShapeDtypeStruct((tm, d), F32),
            jax.ShapeDtypeStruct((d, ff), BF16),
            jax.ShapeDtypeStruct((d, ff), BF16),
            jax.ShapeDtypeStruct((ff, d), BF16),
        ],
        scratch_shapes=[pltpu.VMEM((tm, d), BF16)],
        compiler_params=_params(("arbitrary",)),
        name="ffn_head",
    )(x, norm_w.reshape(1, d), w_in, w_in, w_out)


def _ffn_kernel(x_hbm, nw_ref, wg_ref, wu_ref, wo_ref, fw_ref, *rest, nf, ncast, has_head,
                final_norm):
    head_hbm = rest[0] if has_head else None
    rest = rest[has_head:]
    cast_in = rest[:ncast]
    o_ref = rest[ncast]
    cast_out = rest[ncast + 1:2 * ncast + 1]
    h_ref, xbuf, sem = rest[2 * ncast + 1:]
    i, f = pl.program_id(0), pl.program_id(1)

    def side_casts():
        for src, dst in zip(cast_in, cast_out):
            dst[...] = src[...].astype(BF16)

    def first_step(x):
        side_casts()
        h = _rms(x, nw_ref[...]).astype(BF16)
        h_ref[...] = h
        o_ref[...] = x + _swiglu_half(h, wg_ref[...], wu_ref[...], wo_ref[...])

    _row_tile_prefetch(x_hbm, xbuf, sem, first_step, first=int(has_head))

    last = nf - 1 if final_norm else nf

    @pl.when((f > 0) & (f < last) & (i >= int(has_head)))
    def _():
        side_casts()
        o_ref[...] += _swiglu_half(h_ref[...], wg_ref[...], wu_ref[...], wo_ref[...])

    if has_head:
        @pl.when(i == 0)
        def _():
            side_casts()

        @pl.when((i == 0) & (f == 0))
        def _():
            copy = pltpu.make_async_copy(head_hbm, o_ref, sem)
            copy.start()
            copy.wait()

    if final_norm:
        @pl.when(f == nf - 1)
        def _():
            side_casts()
            o = o_ref[...] + _swiglu_half(h_ref[...], wg_ref[...], wu_ref[...], wo_ref[...])
            o_ref[...] = _rms(o, fw_ref[...])


def _ffn(x, norm_w, w_gate, w_up, w_out, final_w, *, final_norm, head=None, cast=(), tm=FFN_TM, tf=512):
    t, d = x.shape
    ff = w_out.shape[0]
    tm, tf = min(tm, t), min(tf, ff)
    nf = ff // tf
    assert t % tm == 0 and ff % tf == 0 and nf >= 2
    assert head is None or (head.shape == (tm, d) and t // tm >= 2 and not final_norm)
    up0 = nf if w_up.shape[1] == 2 * ff else 0
    hold = (lambda i, f: f) if head is None else (lambda i, f: jnp.where(i == 0, 0, f))
    plans, cast_w, cast_specs, cast_shapes = _cast_jobs(cast, t // tm, nf)
    outs = pl.pallas_call(
        functools.partial(_ffn_kernel, nf=nf, ncast=len(cast_w), has_head=head is not None,
                          final_norm=final_norm),
        grid=(t // tm, nf),
        in_specs=[
            pl.BlockSpec(memory_space=pl.ANY),
            pl.BlockSpec((1, d), lambda i, f: (0, 0)),
            pl.BlockSpec((d, tf), lambda i, f: (0, hold(i, f))),
            pl.BlockSpec((d, tf), lambda i, f: (0, hold(i, f) + up0)),
            pl.BlockSpec((tf, d), lambda i, f: (hold(i, f), 0)),
            pl.BlockSpec((1, d), lambda i, f: (0, 0)),
        ] + ([pl.BlockSpec(memory_space=pl.ANY)] if head is not None else []) + cast_specs,
        out_specs=[pl.BlockSpec((tm, d), lambda i, f: (i, 0))] + cast_specs,
        out_shape=[jax.ShapeDtypeStruct((t, d), F32)] + cast_shapes,
        scratch_shapes=[pltpu.VMEM((tm, d), BF16), pltpu.VMEM((tm, d), F32),
                        pltpu.SemaphoreType.DMA(())],
        compiler_params=_params(("arbitrary", "arbitrary")),
        name="ffn",
    )(x, norm_w.reshape(1, d), w_gate, w_up, w_out, final_w.reshape(1, d),
      *([head] if head is not None else []), *cast_w)
    return outs[0], _merge_cast_results(cast, plans, outs[1:])


def _sigmoid(x):
    return 0.5 * jnp.tanh(0.5 * x) + 0.5


def _proj_kernel(x_hbm, nw_ref, w_ref, cos_ref, sina_ref, sinb_ref, lbraw_ref, *rest, ncast, nsteps,
                 layer):
    cast_in = rest[:ncast]
    o_ref, fl_ref = rest[ncast:ncast + 2]
    cast_out = rest[ncast + 2:2 * ncast + 2]
    h_ref, xbuf, sem = rest[2 * ncast + 2:]
    j = pl.program_id(1)
    tn = PROJ_TN

    def project(h, width):
        for src, dst in zip(cast_in, cast_out):
            dst[...] = src[...].astype(BF16)
        return jnp.dot(h, w_ref[:, 0:width], preferred_element_type=F32)

    def store_rotated(r, c_base, scale):
        half = ROT_DIM // 2
        cos, sina, sinb = (t[...] if scale == 1.0 else t[...] * scale
                           for t in (cos_ref, sina_ref, sinb_ref))
        for c0 in range(0, r.shape[1], LANES):
            rc = r[:, c0:c0 + LANES]
            rot = rc * cos + pltpu.roll(rc, LANES - half, 1) * sina + pltpu.roll(rc, half, 1) * sinb
            o_ref[:, c_base + c0:c_base + c0 + LANES] = rot.astype(BF16)

    def first_step(x):
        h = _rms(x, nw_ref[...]).astype(BF16)
        h_ref[...] = h
        r = project(h, 2 * tn)
        store_rotated(r[:, :tn], 0, ATTN_HEAD_DIM ** -0.5 * math.log2(math.e))
        store_rotated(r[:, tn:], tn, 1.0)

    assert (COL_QA, COL_KA, COL_VA, COL_QR, COL_FR, COL_IR, COL_GR) == (0, 1, 2, 3, 4, 5, 6)
    _row_tile_prefetch(x_hbm, xbuf, sem, first_step)

    def silu(x):
        return x * _sigmoid(x)

    @pl.when(j == COL_QR // 2)
    def _():
        r = project(h_ref[...], 2 * tn)
        o_ref[:, 0:tn] = r[:, :tn].astype(BF16)
        o_ref[:, tn:2 * tn] = silu(r[:, tn:]).astype(BF16)

    @pl.when(j == COL_FR // 2)
    def _():
        r = project(h_ref[...], 2 * tn)
        raw = lbraw_ref[...]
        e = jnp.exp(raw - jnp.max(raw, axis=0, keepdims=True))
        lb = jnp.sum(e[0:layer + 1], axis=0, keepdims=True) / jnp.sum(e, axis=0, keepdims=True)
        fl_ref[...] = jnp.log2(lb + (1.0 - lb) * _sigmoid(r[:, :tn]))
        o_ref[...] = r.astype(BF16)

    @pl.when(j == COL_GR // 2)
    def _():
        r = project(h_ref[...], 2 * tn)
        o_ref[:, 0:tn] = silu(r[:, :tn]).astype(BF16)
        o_ref[:, tn:2 * tn] = r[:, tn:].astype(BF16)

    @pl.when((j > COL_GR // 2) & (j < nsteps - 1))
    def _():
        o_ref[...] = project(h_ref[...], 2 * tn).astype(BF16)

    @pl.when(j == nsteps - 1)
    def _():
        o_ref[:, 0:tn] = project(h_ref[...], tn).astype(BF16)


def _rope_tables(seq):
    half = ROT_DIM // 2
    inv_freq = ROPE_THETA ** (-jnp.arange(0, ROT_DIM, 2, dtype=F32) / ROT_DIM)
    ang = jnp.arange(seq).astype(F32)[:, None] * inv_freq[None, :]
    cos8, sin8 = jnp.cos(ang), jnp.sin(ang)
    pad = ATTN_HEAD_DIM - ROT_DIM
    ones = jnp.ones((seq, pad), F32)
    zeros = jnp.zeros((seq, pad), F32)
    z8 = jnp.zeros((seq, half), F32)
    cos = jnp.concatenate([cos8, cos8, ones], axis=1)
    sina = jnp.concatenate([-sin8, z8, zeros], axis=1)
    sinb = jnp.concatenate([z8, sin8, zeros], axis=1)
    rep = LANES // ATTN_HEAD_DIM
    return tuple(jnp.tile(a, (1, rep)) for a in (cos, sina, sinb))


def _proj(x, norm_w, w, lb_raw, seq, *, layer=0, cast=(), tm=1024):
    t, d = x.shape
    n = w.shape[1]
    tm = min(tm, seq)
    tn = PROJ_TN
    assert t % tm == 0 and seq % tm == 0 and n % tn == 0 and (n // tn) % 2 == 1
    nsteps = (n // tn + 1) // 2
    spt = seq // tm
    cos, sina, sinb = _rope_tables(seq)
    tab_spec = pl.BlockSpec((tm, LANES), lambda i, j: (i % spt, 0))
    plans, cast_w, cast_specs, cast_shapes = _cast_jobs(cast, t // tm, nsteps)
    outs = pl.pallas_call(
        functools.partial(_proj_kernel, ncast=len(cast_w), nsteps=nsteps, layer=layer),
        grid=(t // tm, nsteps),
        in_specs=[
            pl.BlockSpec(memory_space=pl.ANY),
            pl.BlockSpec((1, d), lambda i, j: (0, 0)),
            pl.BlockSpec((d, 2 * tn), lambda i, j: (0, j)),
            tab_spec, tab_spec, tab_spec,
            pl.BlockSpec((lb_raw.shape[0], tn), lambda i, j: (0, 0)),
        ] + cast_specs,
        out_specs=[
            pl.BlockSpec((tm, 2 * tn), lambda i, j: (i, j)),
            pl.BlockSpec((tm, tn), lambda i, j: (i, 0)),
        ] + cast_specs,
        out_shape=[
            jax.ShapeDtypeStruct((t, n), BF16),
            jax.ShapeDtypeStruct((t, tn), F32),
        ] + cast_shapes,
        scratch_shapes=[pltpu.VMEM((tm, d), BF16), pltpu.VMEM((tm, d), F32),
                        pltpu.SemaphoreType.DMA(())],
        compiler_params=_params(("arbitrary", "arbitrary"), PROJ_VMEM_LIMIT),
        name="proj",
    )(x, norm_w.reshape(1, d), w, cos, sina, sinb, lb_raw.astype(F32), *cast_w)
    return outs[0], outs[1], _merge_cast_results(cast, plans, outs[2:])


def _attn_kernel(q_ref, k_ref, v_ref, lq1_ref, lk1_ref, lq2_ref, lk2_ref, sw_ref, o_ref,
                 vx_ref, *, tq, lambda_init):
    hd = ATTN_HEAD_DIM
    w = HEAD_W
    hq = tq // 2
    seq = q_ref.shape[0]
    nt = (((1,), (1,)), ((), ()))
    vx_ref[:, 0:w] = v_ref[...]
    vx_ref[:, w:2 * w] = jnp.ones(v_ref.shape, BF16)
    lam = (jnp.exp(jnp.sum(lq1_ref[...] * lk1_ref[...], axis=-1, keepdims=True))
           - jnp.exp(jnp.sum(lq2_ref[...] * lk2_ref[...], axis=-1, keepdims=True))
           + lambda_init)
    lane = lax.broadcasted_iota(jnp.int32, (hq, w), 1)
    row = lax.broadcasted_iota(jnp.int32, (hq, hq), 0)
    col = lax.broadcasted_iota(jnp.int32, (hq, hq), 1)
    tri = row >= col
    tri2 = jnp.concatenate([tri, tri], axis=0)
    tri2_all = jnp.concatenate([tri2, jnp.ones((2 * hq, hq), jnp.bool_)], axis=0)

    def update(m, acc, s, vx):
        m_new = jnp.maximum(m, jnp.max(s, axis=-1, keepdims=True))
        alpha = jnp.exp2(m - m_new)
        p = jnp.exp2(s - jnp.concatenate([m_new] * (s.shape[1] // w), axis=1))
        acc = (jnp.concatenate([alpha, alpha], axis=1) * acc
               + jnp.dot(p.astype(BF16), vx, preferred_element_type=F32))
        return m_new, acc

    for qi in range(seq // tq):
        parts = []
        for r0 in (qi * tq, qi * tq + hq):
            q = q_ref[r0:r0 + hq, :]
            zero = jnp.zeros_like(q)
            parts += [jnp.where(lane < hd, q, zero), jnp.where(lane >= hd, q, zero)]
        qs = jnp.concatenate(parts, axis=0)
        m = jnp.full((2 * tq, w), -jnp.inf, F32)
        acc = jnp.zeros((2 * tq, 2 * w), F32)
        for kb in range(2 * qi):
            keys = slice(kb * hq, (kb + 1) * hq)
            s = lax.dot_general(qs, k_ref[keys, :], nt, preferred_element_type=F32)
            m, acc = update(m, acc, s, vx_ref[keys, :])
        keys = slice(qi * tq, qi * tq + hq)
        s = lax.dot_general(qs, k_ref[keys, :], nt, preferred_element_type=F32)
        m, acc = update(m, acc, jnp.where(tri2_all, s, -jnp.inf), vx_ref[keys, :])
        keys = slice(qi * tq + hq, (qi + 1) * tq)
        s = lax.dot_general(qs[tq:, :], k_ref[keys, :], nt, preferred_element_type=F32)
        m_hi, acc_hi = update(m[tq:, :], acc[tq:, :], jnp.where(tri2, s, -jnp.inf), vx_ref[keys, :])
        acc = jnp.concatenate([acc[:tq, :], acc_hi], axis=0)
        o = acc[:, 0:w] / acc[:, w:2 * w]
        o = jnp.concatenate([o[0:hq, :] - lam * o[hq:tq, :],
                             o[tq:tq + hq, :] - lam * o[tq + hq:2 * tq, :]], axis=0)
        o_ref[qi * tq:(qi + 1) * tq, :] = (_rms(o, sw_ref[...]) * (1.0 - lambda_init)).astype(BF16)


def _attn(proj, lq1, lk1, lq2, lk2, subln_w, batch, seq, *, layer=0, tq=512):
    t = proj.shape[0]
    tq = min(tq, seq)
    assert seq % tq == 0 and tq % HEAD_W == 0
    heads = ATTN_HEADS
    gpt = PROJ_TN // HEAD_W
    lambda_init = 0.8 - 0.6 * math.exp(-0.3 * layer)
    vec = lambda a: a.reshape(1, -1).astype(F32)
    small = lambda w: pl.BlockSpec((1, w), lambda b, h: (0, 0))
    blk = lambda col: pl.BlockSpec((seq, HEAD_W), lambda b, h: (b, col * gpt + h))
    return pl.pallas_call(
        functools.partial(_attn_kernel, tq=tq, lambda_init=lambda_init),
        grid=(batch, heads),
        in_specs=[
            blk(COL_QA), blk(COL_KA), blk(COL_VA),
            small(ATTN_HEAD_DIM), small(ATTN_HEAD_DIM), small(ATTN_HEAD_DIM), small(ATTN_HEAD_DIM),
            small(HEAD_W),
        ],
        out_specs=pl.BlockSpec((seq, HEAD_W), lambda b, h: (b, h)),
        out_shape=jax.ShapeDtypeStruct((t, heads * HEAD_W), BF16),
        scratch_shapes=[pltpu.VMEM((seq, 2 * HEAD_W), BF16)],
        compiler_params=_params(("parallel", "parallel")),
        name="attn",
    )(proj, proj, proj, vec(lq1), vec(lk1), vec(lq2), vec(lk2), vec(subln_w))


def _hgrn2_kernel(q_ref, lf_ref, i_ref, g_ref, gw_ref, o_ref,
                  g_s, kf_s, qd_s, intra_s, upd_s, eg_s, st_s, *, nchunks):
    c = REC_CHUNK
    nt = (((1,), (1,)), ((), ()))
    tn = (((0,), (0,)), ((), ()))
    gw = gw_ref[...]
    seq = nchunks * c
    grp = REC_GROUP
    gr = grp * c
    ngroups = nchunks // grp

    g = lf_ref[...]
    kf_s[...] = 1.0 - jnp.exp2(g)
    pos = lax.broadcasted_iota(jnp.int32, (seq, HEAD_W), 0) % c
    shift = 1
    while shift < c:
        g = g + jnp.where(pos >= shift, pltpu.roll(g, shift, 0), 0.0)
        shift *= 2
    g_s[...] = g
    factorable = jnp.min(g) >= -REC_SAFE_SPAN_LOG2

    def chunk_last(gcum):
        n = gcum.shape[0] // c
        return jnp.concatenate(
            [jnp.broadcast_to(gcum[(j + 1) * c - 1:(j + 1) * c, :], (c, HEAD_W)) for j in range(n)],
            axis=0)

    for gi in range(ngroups):
        rows = slice(gi * gr, (gi + 1) * gr)
        gcum = g_s[rows, :]
        qd_s[rows, :] = (q_ref[rows, :].astype(F32) * jnp.exp2(gcum)).astype(BF16)
        glast = chunk_last(gcum)
        kd = (kf_s[rows, :] * jnp.exp2(glast - gcum)).astype(BF16)
        v = i_ref[rows, :]
        for j in range(grp):
            ci = gi * grp + j
            sub = slice(j * c, (j + 1) * c)
            upd_s[ci] = lax.dot_general(v[sub, :], kd[sub, :], tn, preferred_element_type=F32)
            eg_s[ci] = jnp.exp2(glast[j * c:j * c + SUBLANES, :])

    @pl.when(factorable)
    def _():
        row = lax.broadcasted_iota(jnp.int32, (gr, gr), 0)
        col = lax.broadcasted_iota(jnp.int32, (gr, gr), 1)
        keep = (row >= col) & ((row // c) == (col // c))
        for gi in range(ngroups):
            rows = slice(gi * gr, (gi + 1) * gr)
            ku = (kf_s[rows, :] * jnp.exp2(-g_s[rows, :])).astype(BF16)
            a = lax.dot_general(qd_s[rows, :], ku, nt, preferred_element_type=F32)
            a = jnp.where(keep, a, 0.0).astype(BF16)
            intra_s[rows, :] = jnp.dot(a, i_ref[rows, :], preferred_element_type=F32)

    @pl.when(jnp.logical_not(factorable))
    def _():
        row = lax.broadcasted_iota(jnp.int32, (c, c), 0)
        col = lax.broadcasted_iota(jnp.int32, (c, c), 1)

        def chunk(ci, _):
            rows = pl.ds(pl.multiple_of(ci * c, c), c)
            qf = q_ref[rows, :].astype(F32)
            gcum = g_s[rows, :]

            def column(s, a):
                src = pl.ds(ci * c + s, 1)
                term = qf * kf_s[src, :] * jnp.exp2(jnp.minimum(gcum - g_s[src, :], 0.0))
                return jnp.where(col == s, jnp.sum(term, axis=1, keepdims=True), a)

            a = lax.fori_loop(0, c, column, jnp.zeros((c, c), F32))
            a = jnp.where(row >= col, a, 0.0).astype(BF16)
            intra_s[rows, :] = jnp.dot(a, i_ref[rows, :], preferred_element_type=F32)
            return 0

        lax.fori_loop(0, nchunks, chunk, 0)

    st = jnp.zeros((HEAD_W, HEAD_W), F32)
    for ci in range(nchunks):
        st_s[ci] = st.astype(BF16)
        st = st * jnp.broadcast_to(eg_s[ci][0:1, :], st.shape) + upd_s[ci]

    for gi in range(ngroups):
        rows = slice(gi * gr, (gi + 1) * gr)
        inter = jnp.concatenate(
            [lax.dot_general(qd_s[(gi * grp + j) * c:(gi * grp + j + 1) * c, :], st_s[gi * grp + j], nt,
                             preferred_element_type=F32) for j in range(grp)], axis=0)
        o = _rms(inter + intra_s[rows, :], gw)
        o_ref[rows, :] = (o * g_ref[rows, :].astype(F32)).astype(BF16)


def _hgrn2(proj, log2_f, gnorm_w, batch, seq):
    t = proj.shape[0]
    heads = log2_f.shape[1] // HEAD_W
    gpt = PROJ_TN // HEAD_W
    assert seq % (REC_CHUNK * REC_GROUP) == 0
    nchunks = seq // REC_CHUNK
    blk = lambda col: pl.BlockSpec((seq, HEAD_W), lambda b, h: (b, col * gpt + h))
    return pl.pallas_call(
        functools.partial(_hgrn2_kernel, nchunks=nchunks),
        grid=(batch, heads),
        in_specs=[
            blk(COL_QR),
            pl.BlockSpec((seq, HEAD_W), lambda b, h: (b, h)),
            blk(COL_IR),
            blk(COL_GR),
            pl.BlockSpec((1, HEAD_W), lambda b, h: (0, 0)),
        ],
        out_specs=pl.BlockSpec((seq, HEAD_W), lambda b, h: (b, h)),
        out_shape=jax.ShapeDtypeStruct((t, heads * HEAD_W), BF16),
        scratch_shapes=[
            pltpu.VMEM((seq, HEAD_W), F32),
            pltpu.VMEM((seq, HEAD_W), F32),
            pltpu.VMEM((seq, HEAD_W), BF16),
            pltpu.VMEM((seq, HEAD_W), F32),
            pltpu.VMEM((nchunks, HEAD_W, HEAD_W), F32),
            pltpu.VMEM((nchunks, SUBLANES, HEAD_W), F32),
            pltpu.VMEM((nchunks, HEAD_W, HEAD_W), BF16),
        ],
        compiler_params=_params(("parallel", "parallel")),
        name="hgrn2",
    )(proj, log2_f, proj, proj, gnorm_w.reshape(1, -1).astype(F32))


def _merge_kernel(x_ref, ya_ref, yr_ref, ga0_ref, ga1_ref, gb0_ref, gb1_ref, wa_ref, wr_ref, wo_ref,
                  o_ref):
    pa = jnp.dot(ya_ref[...], wa_ref[...], preferred_element_type=F32)
    pr = jnp.dot(yr_ref[...], wr_ref[...], preferred_element_type=F32)
    ga = jnp.concatenate([ga0_ref[...], ga1_ref[...]], axis=1).astype(F32)
    gb = jnp.concatenate([gb0_ref[...], gb1_ref[...]], axis=1).astype(F32)
    merged = jax.nn.sigmoid(ga) * pa + jax.nn.sigmoid(gb) * pr
    o_ref[...] = x_ref[...] + jnp.dot(merged.astype(BF16), wo_ref[...], preferred_element_type=F32)


def _merge(x, ya, yr, proj, wa, wr, wo, *, tm=512):
    t, d = x.shape
    tm = min(tm, t)
    wa_k, wr_k = wa.shape[0], wr.shape[0]
    assert d == 2 * PROJ_TN
    const = lambda shape: pl.BlockSpec(shape, lambda i: (0, 0), pipeline_mode=pl.Buffered(1))
    gate = lambda col: pl.BlockSpec((tm, PROJ_TN), lambda i: (i, col))
    return pl.pallas_call(
        _merge_kernel,
        grid=(t // tm,),
        in_specs=[
            pl.BlockSpec((tm, d), lambda i: (i, 0)),
            pl.BlockSpec((tm, wa_k), lambda i: (i, 0)),
            pl.BlockSpec((tm, wr_k), lambda i: (i, 0)),
            gate(COL_GA), gate(COL_GA + 1), gate(COL_GB), gate(COL_GB + 1),
            const((wa_k, d)), const((wr_k, d)), const((d, d)),
        ],
        out_specs=pl.BlockSpec((tm, d), lambda i: (i, 0)),
        out_shape=jax.ShapeDtypeStruct((t, d), F32),
        compiler_params=_params(("parallel",)),
        name="merge",
    )(x, ya, yr, proj, proj, proj, proj, wa, wr, wo)


def kernel(x, ffn1_norm, ffn1_in, ffn1_out, mix_norm, w_in, lambda_q1, lambda_k1, lambda_q2,
           lambda_k2, attn_subln, rec_lb_raw, rec_gnorm, w_proj_attn, w_proj_rec, w_out,
           ffn2_norm, ffn2_in, ffn2_out, final_norm):
    batch, seq, d = x.shape
    depth = ffn1_in.shape[0]
    bf = lambda w: w.astype(BF16)
    xt = x.reshape(batch * seq, d)
    for l in range(depth):
        tm = min(FFN_TM, batch * seq)
        if batch * seq >= 2 * tm:
            head, f1_gate_b, f1_up_b, f1_out_b = _ffn_head(xt, ffn1_norm[l], ffn1_in[l], ffn1_out[l], tm=tm)
        else:
            head, f1_gate_b, f1_up_b, f1_out_b = None, bf(ffn1_in[l]), bf(ffn1_in[l]), bf(ffn1_out[l])
        xt, (w_in_b, f2_in_b, f2_out_b) = _ffn(
            xt, ffn1_norm[l], f1_gate_b, f1_up_b, f1_out_b, final_norm, final_norm=False, head=head,
            cast=(w_in[l], ffn2_in[l], ffn2_out[l]), tm=tm)
        proj, log2_f, (wa_b, wr_b, wo_b) = _proj(xt, mix_norm[l], w_in_b, rec_lb_raw, seq, layer=l,
                                                 cast=(w_proj_attn[l], w_proj_rec[l], w_out[l]))
        ya = _attn(proj, lambda_q1[l], lambda_k1[l], lambda_q2[l], lambda_k2[l], attn_subln[l],
                   batch, seq, layer=l)
        yr = _hgrn2(proj, log2_f, rec_gnorm[l], batch, seq)
        xt = _merge(xt, ya, yr, proj, wa_b, wr_b, wo_b)
        xt, _ = _ffn(xt, ffn2_norm[l], f2_in_b, f2_in_b, f2_out_b, final_norm,
                     final_norm=(l == depth - 1), tm=tm)
    return xt.reshape(batch, seq, d)
```

```python
import functools
import math

import jax
import jax.numpy as jnp
from jax import lax
from jax.experimental import pallas as pl
from jax.experimental.pallas import tpu as pltpu

F32 = jnp.float32
BF16 = jnp.bfloat16

EPS = 1e-6
ATTN_HEADS = 8
ATTN_HEAD_DIM = 64
ROPE_THETA = 500000.0
ROT_DIM = ATTN_HEAD_DIM // 4
HEAD_W = 128
REC_CHUNK = 64
REC_GROUP = 4
REC_SAFE_SPAN_LOG2 = 115.0
LANES = 128
SUBLANES = 8
BF16_ROWS = 16
FFN_TM = 1024
PROJ_TN = 1024
COL_QA, COL_KA, COL_VA, COL_QR, COL_FR, COL_IR, COL_GR, COL_GA, COL_GB = 0, 1, 2, 3, 4, 5, 6, 7, 9
VMEM_LIMIT = 52 * 1024 * 1024
PROJ_VMEM_LIMIT = 61 * 1024 * 1024
CAST_BLOCK_BYTES = 1024 * 1024


def _rms(x, w):
    return x * lax.rsqrt(jnp.mean(x * x, axis=-1, keepdims=True) + EPS) * w


def _params(sem, vmem_limit=VMEM_LIMIT):
    return pltpu.CompilerParams(dimension_semantics=sem, vmem_limit_bytes=vmem_limit)


def _row_tile_prefetch(x_hbm, xbuf, sem, consume, first=0):
    i, s = pl.program_id(0), pl.program_id(1)
    tm = xbuf.shape[0]

    def copy(tile):
        return pltpu.make_async_copy(x_hbm.at[pl.ds(pl.multiple_of(tile * tm, tm), tm), :], xbuf, sem)

    if first == 0:
        @pl.when((s == 0) & (i == 0))
        def _():
            copy(0).start()

    @pl.when((s == 0) & (i >= first))
    def _():
        copy(i).wait()
        consume(xbuf[...])

    @pl.when((s == 1) & (i + 1 >= first) & (i + 1 < pl.num_programs(0)))
    def _():
        copy(i + 1).start()


def _cast_plan(shape, ni, ninner):
    r, c = shape
    small = lambda rows, cols: rows * cols * 4 <= CAST_BLOCK_BYTES
    for nsteps, step in ((ni * ninner, lambda i, s: i * ninner + s), (ni, lambda i, s: i)):
        if r % nsteps == 0 and (r // nsteps) % BF16_ROWS == 0 and small(r // nsteps, c):
            return (r // nsteps, c), (lambda i, s, step=step: (step(i, s), 0))
        ncol = c // LANES
        if c % LANES == 0 and nsteps % ncol == 0:
            split = nsteps // ncol
            if r % split == 0 and (r // split) % BF16_ROWS == 0 and small(r // split, LANES):
                return ((r // split, LANES),
                        (lambda i, s, step=step, split=split: (step(i, s) % split, step(i, s) // split)))
    return None


def _cast_jobs(weights, ni, ninner):
    plans = [_cast_plan(w.shape, ni, ninner) for w in weights]
    ride = [(w, p) for w, p in zip(weights, plans) if p is not None]
    in_specs = [pl.BlockSpec(p[0], p[1]) for _, p in ride]
    out_shapes = [jax.ShapeDtypeStruct(w.shape, BF16) for w, _ in ride]
    return plans, [w for w, _ in ride], in_specs, out_shapes


def _merge_cast_results(weights, plans, cast_outs):
    outs = iter(cast_outs)
    return [next(outs) if p is not None else w.astype(BF16) for w, p in zip(weights, plans)]


def _swiglu_half(h, wg, wu, wo):
    g = jnp.dot(h, wg, preferred_element_type=F32)
    u = jnp.dot(h, wu, preferred_element_type=F32)
    a = (g * jax.nn.sigmoid(g) * u * 0.5).astype(BF16)
    return jnp.dot(a, wo, preferred_element_type=F32)


def _ffn_head_kernel(x_ref, nw_ref, wg_ref, wu_ref, wo_ref, o_ref, wg_b, wu_b, wo_b, h_ref):
    @pl.when(pl.program_id(0) == 0)
    def _():
        x = x_ref[...]
        h_ref[...] = _rms(x, nw_ref[...]).astype(BF16)
        o_ref[...] = x

    wg_b[...] = wg_ref[...].astype(BF16)
    wu_b[...] = wu_ref[...].astype(BF16)
    wo_b[...] = wo_ref[...].astype(BF16)
    o_ref[...] += _swiglu_half(h_ref[...], wg_b[...], wu_b[...], wo_b[...])


def _ffn_head(x, norm_w, w_in, w_out, *, tm, tf=256):
    t, d = x.shape
    ff = w_out.shape[0]
    nf = ff // tf
    assert ff % tf == 0 and t % tm == 0
    return pl.pallas_call(
        _ffn_head_kernel,
        grid=(nf,),
        in_specs=[
            pl.BlockSpec((tm, d), lambda f: (0, 0), pipeline_mode=pl.Buffered(1)),
            pl.BlockSpec((1, d), lambda f: (0, 0)),
            pl.BlockSpec((d, tf), lambda f: (0, f)),
            pl.BlockSpec((d, tf), lambda f: (0, f + nf)),
            pl.BlockSpec((tf, d), lambda f: (f, 0)),
        ],
        out_specs=[
            pl.BlockSpec((tm, d), lambda f: (0, 0)),
            pl.BlockSpec((d, tf), lambda f: (0, f)),
            pl.BlockSpec((d, tf), lambda f: (0, f)),
            pl.BlockSpec((tf, d), lambda f: (f, 0)),
        ],
        out_shape=[
            jax.ShapeDtypeStruct((tm, d), F32),
            jax.ShapeDtypeStruct((d, ff), BF16),
            jax.ShapeDtypeStruct((d, ff), BF16),
            jax.ShapeDtypeStruct((ff, d), BF16),
        ],
        scratch_shapes=[pltpu.VMEM((tm, d), BF16)],
        compiler_params=_params(("arbitrary",)),
        name="ffn_head",
    )(x, norm_w.reshape(1, d), w_in, w_in, w_out)


def _ffn_kernel(x_hbm, nw_ref, wg_ref, wu_ref, wo_ref, fw_ref, *rest, nf, ncast, has_head,
                final_norm):
    head_hbm = rest[0] if has_head else None
    rest = rest[has_head:]
    cast_in = rest[:ncast]
    o_ref = rest[ncast]
    cast_out = rest[ncast + 1:2 * ncast + 1]
    h_ref, xbuf, sem = rest[2 * ncast + 1:]
    i, f = pl.program_id(0), pl.program_id(1)

    def side_casts():
        for src, dst in zip(cast_in, cast_out):
            dst[...] = src[...].astype(BF16)

    def first_step(x):
        side_casts()
        h = _rms(x, nw_ref[...]).astype(BF16)
        h_ref[...] = h
        o_ref[...] = x + _swiglu_half(h, wg_ref[...], wu_ref[...], wo_ref[...])

    _row_tile_prefetch(x_hbm, xbuf, sem, first_step, first=int(has_head))

    last = nf - 1 if final_norm else nf

    @pl.when((f > 0) & (f < last) & (i >= int(has_head)))
    def _():
        side_casts()
        o_ref[...] += _swiglu_half(h_ref[...], wg_ref[...], wu_ref[...], wo_ref[...])

    if has_head:
        @pl.when(i == 0)
        def _():
            side_casts()

        @pl.when((i == 0) & (f == 0))
        def _():
            copy = pltpu.make_async_copy(head_hbm, o_ref, sem)
            copy.start()
            copy.wait()

    if final_norm:
        @pl.when(f == nf - 1)
        def _():
            side_casts()
            o = o_ref[...] + _swiglu_half(h_ref[...], wg_ref[...], wu_ref[...], wo_ref[...])
            o_ref[...] = _rms(o, fw_ref[...])


def _ffn(x, norm_w, w_gate, w_up, w_out, final_w, *, final_norm, head=None, cast=(), tm=FFN_TM, tf=512):
    t, d = x.shape
    ff = w_out.shape[0]
    tm, tf = min(tm, t), min(tf, ff)
    nf = ff // tf
    assert t % tm == 0 and ff % tf == 0 and nf >= 2
    assert head is None or (head.shape == (tm, d) and t // tm >= 2 and not final_norm)
    up0 = nf if w_up.shape[1] == 2 * ff else 0
    hold = (lambda i, f: f) if head is None else (lambda i, f: jnp.where(i == 0, 0, f))
    plans, cast_w, cast_specs, cast_shapes = _cast_jobs(cast, t // tm, nf)
    outs = pl.pallas_call(
        functools.partial(_ffn_kernel, nf=nf, ncast=len(cast_w), has_head=head is not None,
                          final_norm=final_norm),
        grid=(t // tm, nf),
        in_specs=[
            pl.BlockSpec(memory_space=pl.ANY),
            pl.BlockSpec((1, d), lambda i, f: (0, 0)),
            pl.BlockSpec((d, tf), lambda i, f: (0, hold(i, f))),
            pl.BlockSpec((d, tf), lambda i, f: (0, hold(i, f) + up0)),
            pl.BlockSpec((tf, d), lambda i, f: (hold(i, f), 0)),
            pl.BlockSpec((1, d), lambda i, f: (0, 0)),
        ] + ([pl.BlockSpec(memory_space=pl.ANY)] if head is not None else []) + cast_specs,
        out_specs=[pl.BlockSpec((tm, d), lambda i, f: (i, 0))] + cast_specs,
        out_shape=[jax.ShapeDtypeStruct((t, d), F32)] + cast_shapes,
        scratch_shapes=[pltpu.VMEM((tm, d), BF16), pltpu.VMEM((tm, d), F32),
                        pltpu.SemaphoreType.DMA(())],
        compiler_params=_params(("arbitrary", "arbitrary")),
        name="ffn",
    )(x, norm_w.reshape(1, d), w_gate, w_up, w_out, final_w.reshape(1, d),
      *([head] if head is not None else []), *cast_w)
    return outs[0], _merge_cast_results(cast, plans, outs[1:])


def _proj_kernel(x_hbm, nw_ref, w_ref, cos_ref, sina_ref, sinb_ref, *rest, ncast, nsteps):
    cast_in = rest[:ncast]
    o_ref, fl_ref = rest[ncast:ncast + 2]
    cast_out = rest[ncast + 2:2 * ncast + 2]
    h_ref, xbuf, sem = rest[2 * ncast + 2:]
    j = pl.program_id(1)
    tn = PROJ_TN

    def project(h, width):
        for src, dst in zip(cast_in, cast_out):
            dst[...] = src[...].astype(BF16)
        return jnp.dot(h, w_ref[:, 0:width], preferred_element_type=F32)

    def store_rotated(r, c_base, scale):
        half = ROT_DIM // 2
        cos, sina, sinb = (t[...] if scale == 1.0 else t[...] * scale
                           for t in (cos_ref, sina_ref, sinb_ref))
        for c0 in range(0, r.shape[1], LANES):
            rc = r[:, c0:c0 + LANES]
            rot = rc * cos + pltpu.roll(rc, LANES - half, 1) * sina + pltpu.roll(rc, half, 1) * sinb
            o_ref[:, c_base + c0:c_base + c0 + LANES] = rot.astype(BF16)

    def first_step(x):
        h = _rms(x, nw_ref[...]).astype(BF16)
        h_ref[...] = h
        r = project(h, 2 * tn)
        store_rotated(r[:, :tn], 0, ATTN_HEAD_DIM ** -0.5 * math.log2(math.e))
        store_rotated(r[:, tn:], tn, 1.0)

    assert (COL_QA, COL_KA) == (0, 1) and COL_FR % 2 == 0
    _row_tile_prefetch(x_hbm, xbuf, sem, first_step)

    @pl.when(j == COL_FR // 2)
    def _():
        r = project(h_ref[...], 2 * tn)
        fl_ref[...] = r[:, :tn]
        o_ref[...] = r.astype(BF16)

    def silu(x):
        return x * (0.5 * jnp.tanh(0.5 * x) + 0.5)

    @pl.when(j == COL_QR // 2)
    def _():
        r = project(h_ref[...], 2 * tn)
        o_ref[:, 0:tn] = r[:, :tn].astype(BF16)
        o_ref[:, tn:2 * tn] = silu(r[:, tn:]).astype(BF16)

    @pl.when(j == COL_GR // 2)
    def _():
        r = project(h_ref[...], 2 * tn)
        o_ref[:, 0:tn] = silu(r[:, :tn]).astype(BF16)
        o_ref[:, tn:2 * tn] = r[:, tn:].astype(BF16)

    @pl.when((j > COL_GR // 2) & (j < nsteps - 1))
    def _():
        o_ref[...] = project(h_ref[...], 2 * tn).astype(BF16)

    @pl.when(j == nsteps - 1)
    def _():
        o_ref[:, 0:tn] = project(h_ref[...], tn).astype(BF16)


def _rope_tables(seq):
    half = ROT_DIM // 2
    inv_freq = ROPE_THETA ** (-jnp.arange(0, ROT_DIM, 2, dtype=F32) / ROT_DIM)
    ang = jnp.arange(seq).astype(F32)[:, None] * inv_freq[None, :]
    cos8, sin8 = jnp.cos(ang), jnp.sin(ang)
    pad = ATTN_HEAD_DIM - ROT_DIM
    ones = jnp.ones((seq, pad), F32)
    zeros = jnp.zeros((seq, pad), F32)
    z8 = jnp.zeros((seq, half), F32)
    cos = jnp.concatenate([cos8, cos8, ones], axis=1)
    sina = jnp.concatenate([-sin8, z8, zeros], axis=1)
    sinb = jnp.concatenate([z8, sin8, zeros], axis=1)
    rep = LANES // ATTN_HEAD_DIM
    return tuple(jnp.tile(a, (1, rep)) for a in (cos, sina, sinb))


def _proj(x, norm_w, w, seq, *, cast=(), tm=1024):
    t, d = x.shape
    n = w.shape[1]
    tm = min(tm, seq)
    tn = PROJ_TN
    assert t % tm == 0 and seq % tm == 0 and n % tn == 0 and (n // tn) % 2 == 1
    nsteps = (n // tn + 1) // 2
    spt = seq // tm
    cos, sina, sinb = _rope_tables(seq)
    tab_spec = pl.BlockSpec((tm, LANES), lambda i, j: (i % spt, 0))
    plans, cast_w, cast_specs, cast_shapes = _cast_jobs(cast, t // tm, nsteps)
    outs = pl.pallas_call(
        functools.partial(_proj_kernel, ncast=len(cast_w), nsteps=nsteps),
        grid=(t // tm, nsteps),
        in_specs=[
            pl.BlockSpec(memory_space=pl.ANY),
            pl.BlockSpec((1, d), lambda i, j: (0, 0)),
            pl.BlockSpec((d, 2 * tn), lambda i, j: (0, j)),
            tab_spec, tab_spec, tab_spec,
        ] + cast_specs,
        out_specs=[
            pl.BlockSpec((tm, 2 * tn), lambda i, j: (i, j)),
            pl.BlockSpec((tm, tn), lambda i, j: (i, 0)),
        ] + cast_specs,
        out_shape=[
            jax.ShapeDtypeStruct((t, n), BF16),
            jax.ShapeDtypeStruct((t, tn), F32),
        ] + cast_shapes,
        scratch_shapes=[pltpu.VMEM((tm, d), BF16), pltpu.VMEM((tm, d), F32),
                        pltpu.SemaphoreType.DMA(())],
        compiler_params=_params(("arbitrary", "arbitrary"), PROJ_VMEM_LIMIT),
        name="proj",
    )(x, norm_w.reshape(1, d), w, cos, sina, sinb, *cast_w)
    return outs[0], outs[1], _merge_cast_results(cast, plans, outs[2:])


def _attn_kernel(q_ref, k_ref, v_ref, lq1_ref, lk1_ref, lq2_ref, lk2_ref, sw_ref, o_ref,
                 vx_ref, *, tq, lambda_init):
    hd = ATTN_HEAD_DIM
    w = HEAD_W
    hq = tq // 2
    seq = q_ref.shape[0]
    nt = (((1,), (1,)), ((), ()))
    vx_ref[:, 0:w] = v_ref[...]
    vx_ref[:, w:2 * w] = jnp.ones(v_ref.shape, BF16)
    lam = (jnp.exp(jnp.sum(lq1_ref[...] * lk1_ref[...], axis=-1, keepdims=True))
           - jnp.exp(jnp.sum(lq2_ref[...] * lk2_ref[...], axis=-1, keepdims=True))
           + lambda_init)
    lane = lax.broadcasted_iota(jnp.int32, (hq, w), 1)
    row = lax.broadcasted_iota(jnp.int32, (hq, hq), 0)
    col = lax.broadcasted_iota(jnp.int32, (hq, hq), 1)
    tri = row >= col
    tri2 = jnp.concatenate([tri, tri], axis=0)
    tri2_all = jnp.concatenate([tri2, jnp.ones((2 * hq, hq), jnp.bool_)], axis=0)

    def update(m, acc, s, vx):
        m_new = jnp.maximum(m, jnp.max(s, axis=-1, keepdims=True))
        alpha = jnp.exp2(m - m_new)
        p = jnp.exp2(s - jnp.concatenate([m_new] * (s.shape[1] // w), axis=1))
        acc = (jnp.concatenate([alpha, alpha], axis=1) * acc
               + jnp.dot(p.astype(BF16), vx, preferred_element_type=F32))
        return m_new, acc

    for qi in range(seq // tq):
        parts = []
        for r0 in (qi * tq, qi * tq + hq):
            q = q_ref[r0:r0 + hq, :]
            zero = jnp.zeros_like(q)
            parts += [jnp.where(lane < hd, q, zero), jnp.where(lane >= hd, q, zero)]
        qs = jnp.concatenate(parts, axis=0)
        m = jnp.full((2 * tq, w), -jnp.inf, F32)
        acc = jnp.zeros((2 * tq, 2 * w), F32)
        for kb in range(2 * qi):
            keys = slice(kb * hq, (kb + 1) * hq)
            s = lax.dot_general(qs, k_ref[keys, :], nt, preferred_element_type=F32)
            m, acc = update(m, acc, s, vx_ref[keys, :])
        keys = slice(qi * tq, qi * tq + hq)
        s = lax.dot_general(qs, k_ref[keys, :], nt, preferred_element_type=F32)
        m, acc = update(m, acc, jnp.where(tri2_all, s, -jnp.inf), vx_ref[keys, :])
        keys = slice(qi * tq + hq, (qi + 1) * tq)
        s = lax.dot_general(qs[tq:, :], k_ref[keys, :], nt, preferred_element_type=F32)
        m_hi, acc_hi = update(m[tq:, :], acc[tq:, :], jnp.where(tri2, s, -jnp.inf), vx_ref[keys, :])
        acc = jnp.concatenate([acc[:tq, :], acc_hi], axis=0)
        o = acc[:, 0:w] / acc[:, w:2 * w]
        o = jnp.concatenate([o[0:hq, :] - lam * o[hq:tq, :],
                             o[tq:tq + hq, :] - lam * o[tq + hq:2 * tq, :]], axis=0)
        o_ref[qi * tq:(qi + 1) * tq, :] = (_rms(o, sw_ref[...]) * (1.0 - lambda_init)).astype(BF16)


def _attn(proj, lq1, lk1, lq2, lk2, subln_w, batch, seq, *, layer=0, tq=512):
    t = proj.shape[0]
    tq = min(tq, seq)
    assert seq % tq == 0 and tq % HEAD_W == 0
    heads = ATTN_HEADS
    gpt = PROJ_TN // HEAD_W
    lambda_init = 0.8 - 0.6 * math.exp(-0.3 * layer)
    vec = lambda a: a.reshape(1, -1).astype(F32)
    small = lambda w: pl.BlockSpec((1, w), lambda b, h: (0, 0))
    blk = lambda col: pl.BlockSpec((seq, HEAD_W), lambda b, h: (b, col * gpt + h))
    return pl.pallas_call(
        functools.partial(_attn_kernel, tq=tq, lambda_init=lambda_init),
        grid=(batch, heads),
        in_specs=[
            blk(COL_QA), blk(COL_KA), blk(COL_VA),
            small(ATTN_HEAD_DIM), small(ATTN_HEAD_DIM), small(ATTN_HEAD_DIM), small(ATTN_HEAD_DIM),
            small(HEAD_W),
        ],
        out_specs=pl.BlockSpec((seq, HEAD_W), lambda b, h: (b, h)),
        out_shape=jax.ShapeDtypeStruct((t, heads * HEAD_W), BF16),
        scratch_shapes=[pltpu.VMEM((seq, 2 * HEAD_W), BF16)],
        compiler_params=_params(("parallel", "parallel")),
        name="attn",
    )(proj, proj, proj, vec(lq1), vec(lk1), vec(lq2), vec(lk2), vec(subln_w))


def _hgrn2_kernel(q_ref, fl_ref, i_ref, g_ref, lbraw_ref, gw_ref, o_ref,
                  g_s, kf_s, qd_s, intra_s, upd_s, eg_s, st_s, *, nchunks, layer):
    c = REC_CHUNK
    nt = (((1,), (1,)), ((), ()))
    tn = (((0,), (0,)), ((), ()))
    raw = lbraw_ref[...]
    e = jnp.exp(raw - jnp.max(raw, axis=0, keepdims=True))
    lb = jnp.sum(e[0:layer + 1], axis=0, keepdims=True) / jnp.sum(e, axis=0, keepdims=True)
    gw = gw_ref[...]
    seq = nchunks * c
    grp = REC_GROUP
    gr = grp * c
    ngroups = nchunks // grp

    def sigmoid(x):
        return 0.5 * jnp.tanh(0.5 * x) + 0.5

    f = lb + (1.0 - lb) * sigmoid(fl_ref[...])
    kf_s[...] = 1.0 - f
    g = jnp.log2(f)
    pos = lax.broadcasted_iota(jnp.int32, (seq, HEAD_W), 0) % c
    shift = 1
    while shift < c:
        g = g + jnp.where(pos >= shift, pltpu.roll(g, shift, 0), 0.0)
        shift *= 2
    g_s[...] = g
    factorable = jnp.min(g) >= -REC_SAFE_SPAN_LOG2

    def chunk_last(gcum):
        n = gcum.shape[0] // c
        return jnp.concatenate(
            [jnp.broadcast_to(gcum[(j + 1) * c - 1:(j + 1) * c, :], (c, HEAD_W)) for j in range(n)],
            axis=0)

    for gi in range(ngroups):
        rows = slice(gi * gr, (gi + 1) * gr)
        gcum = g_s[rows, :]
        qd_s[rows, :] = (q_ref[rows, :].astype(F32) * jnp.exp2(gcum)).astype(BF16)
        glast = chunk_last(gcum)
        kd = (kf_s[rows, :] * jnp.exp2(glast - gcum)).astype(BF16)
        v = i_ref[rows, :]
        for j in range(grp):
            ci = gi * grp + j
            sub = slice(j * c, (j + 1) * c)
            upd_s[ci] = lax.dot_general(v[sub, :], kd[sub, :], tn, preferred_element_type=F32)
            eg_s[ci] = jnp.exp2(glast[j * c:j * c + SUBLANES, :])

    @pl.when(factorable)
    def _():
        row = lax.broadcasted_iota(jnp.int32, (gr, gr), 0)
        col = lax.broadcasted_iota(jnp.int32, (gr, gr), 1)
        keep = (row >= col) & ((row // c) == (col // c))
        for gi in range(ngroups):
            rows = slice(gi * gr, (gi + 1) * gr)
            ku = (kf_s[rows, :] * jnp.exp2(-g_s[rows, :])).astype(BF16)
            a = lax.dot_general(qd_s[rows, :], ku, nt, preferred_element_type=F32)
            a = jnp.where(keep, a, 0.0).astype(BF16)
            intra_s[rows, :] = jnp.dot(a, i_ref[rows, :], preferred_element_type=F32)

    @pl.when(jnp.logical_not(factorable))
    def _():
        row = lax.broadcasted_iota(jnp.int32, (c, c), 0)
        col = lax.broadcasted_iota(jnp.int32, (c, c), 1)

        def chunk(ci, _):
            rows = pl.ds(pl.multiple_of(ci * c, c), c)
            qf = q_ref[rows, :].astype(F32)
            gcum = g_s[rows, :]

            def column(s, a):
                src = pl.ds(ci * c + s, 1)
                term = qf * kf_s[src, :] * jnp.exp2(jnp.minimum(gcum - g_s[src, :], 0.0))
                return jnp.where(col == s, jnp.sum(term, axis=1, keepdims=True), a)

            a = lax.fori_loop(0, c, column, jnp.zeros((c, c), F32))
            a = jnp.where(row >= col, a, 0.0).astype(BF16)
            intra_s[rows, :] = jnp.dot(a, i_ref[rows, :], preferred_element_type=F32)
            return 0

        lax.fori_loop(0, nchunks, chunk, 0)

    st = jnp.zeros((HEAD_W, HEAD_W), F32)
    for ci in range(nchunks):
        st_s[ci] = st.astype(BF16)
        st = st * jnp.broadcast_to(eg_s[ci][0:1, :], st.shape) + upd_s[ci]

    for gi in range(ngroups):
        rows = slice(gi * gr, (gi + 1) * gr)
        inter = jnp.concatenate(
            [lax.dot_general(qd_s[(gi * grp + j) * c:(gi * grp + j + 1) * c, :], st_s[gi * grp + j], nt,
                             preferred_element_type=F32) for j in range(grp)], axis=0)
        o = _rms(inter + intra_s[rows, :], gw)
        o_ref[rows, :] = (o * g_ref[rows, :].astype(F32)).astype(BF16)


def _hgrn2(proj, fl, lb_raw, gnorm_w, batch, seq, *, layer=0):
    t = proj.shape[0]
    heads = fl.shape[1] // HEAD_W
    gpt = PROJ_TN // HEAD_W
    depth1 = lb_raw.shape[0]
    assert seq % (REC_CHUNK * REC_GROUP) == 0
    nchunks = seq // REC_CHUNK
    blk = lambda col: pl.BlockSpec((seq, HEAD_W), lambda b, h: (b, col * gpt + h))
    return pl.pallas_call(
        functools.partial(_hgrn2_kernel, nchunks=nchunks, layer=layer),
        grid=(batch, heads),
        in_specs=[
            blk(COL_QR),
            pl.BlockSpec((seq, HEAD_W), lambda b, h: (b, h)),
            blk(COL_IR),
            blk(COL_GR),
            pl.BlockSpec((depth1, HEAD_W), lambda b, h: (0, h)),
            pl.BlockSpec((1, HEAD_W), lambda b, h: (0, 0)),
        ],
        out_specs=pl.BlockSpec((seq, HEAD_W), lambda b, h: (b, h)),
        out_shape=jax.ShapeDtypeStruct((t, heads * HEAD_W), BF16),
        scratch_shapes=[
            pltpu.VMEM((seq, HEAD_W), F32),
            pltpu.VMEM((seq, HEAD_W), F32),
            pltpu.VMEM((seq, HEAD_W), BF16),
            pltpu.VMEM((seq, HEAD_W), F32),
            pltpu.VMEM((nchunks, HEAD_W, HEAD_W), F32),
            pltpu.VMEM((nchunks, SUBLANES, HEAD_W), F32),
            pltpu.VMEM((nchunks, HEAD_W, HEAD_W), BF16),
        ],
        compiler_params=_params(("parallel", "parallel")),
        name="hgrn2",
    )(proj, fl, proj, proj, lb_raw.astype(F32), gnorm_w.reshape(1, -1).astype(F32))


def _merge_kernel(x_ref, ya_ref, yr_ref, ga0_ref, ga1_ref, gb0_ref, gb1_ref, wa_ref, wr_ref, wo_ref,
                  o_ref):
    pa = jnp.dot(ya_ref[...], wa_ref[...], preferred_element_type=F32)
    pr = jnp.dot(yr_ref[...], wr_ref[...], preferred_element_type=F32)
    ga = jnp.concatenate([ga0_ref[...], ga1_ref[...]], axis=1).astype(F32)
    gb = jnp.concatenate([gb0_ref[...], gb1_ref[...]], axis=1).astype(F32)
    merged = jax.nn.sigmoid(ga) * pa + jax.nn.sigmoid(gb) * pr
    o_ref[...] = x_ref[...] + jnp.dot(merged.astype(BF16), wo_ref[...], preferred_element_type=F32)


def _merge(x, ya, yr, proj, wa, wr, wo, *, tm=512):
    t, d = x.shape
    tm = min(tm, t)
    wa_k, wr_k = wa.shape[0], wr.shape[0]
    assert d == 2 * PROJ_TN
    const = lambda shape: pl.BlockSpec(shape, lambda i: (0, 0), pipeline_mode=pl.Buffered(1))
    gate = lambda col: pl.BlockSpec((tm, PROJ_TN), lambda i: (i, col))
    return pl.pallas_call(
        _merge_kernel,
        grid=(t // tm,),
        in_specs=[
            pl.BlockSpec((tm, d), lambda i: (i, 0)),
            pl.BlockSpec((tm, wa_k), lambda i: (i, 0)),
            pl.BlockSpec((tm, wr_k), lambda i: (i, 0)),
            gate(COL_GA), gate(COL_GA + 1), gate(COL_GB), gate(COL_GB + 1),
            const((wa_k, d)), const((wr_k, d)), const((d, d)),
        ],
        out_specs=pl.BlockSpec((tm, d), lambda i: (i, 0)),
        out_shape=jax.ShapeDtypeStruct((t, d), F32),
        compiler_params=_params(("parallel",)),
        name="merge",
    )(x, ya, yr, proj, proj, proj, proj, wa, wr, wo)


def kernel(x, ffn1_norm, ffn1_in, ffn1_out, mix_norm, w_in, lambda_q1, lambda_k1, lambda_q2,
           lambda_k2, attn_subln, rec_lb_raw, rec_gnorm, w_proj_attn, w_proj_rec, w_out,
           ffn2_norm, ffn2_in, ffn2_out, final_norm):
    batch, seq, d = x.shape
    depth = ffn1_in.shape[0]
    bf = lambda w: w.astype(BF16)
    xt = x.reshape(batch * seq, d)
    for l in range(depth):
        tm = min(FFN_TM, batch * seq)
        if batch * seq >= 2 * tm:
            head, f1_gate_b, f1_up_b, f1_out_b = _ffn_head(xt, ffn1_norm[l], ffn1_in[l], ffn1_out[l], tm=tm)
        else:
            head, f1_gate_b, f1_up_b, f1_out_b = None, bf(ffn1_in[l]), bf(ffn1_in[l]), bf(ffn1_out[l])
        xt, (w_in_b, f2_in_b, f2_out_b) = _ffn(
            xt, ffn1_norm[l], f1_gate_b, f1_up_b, f1_out_b, final_norm, final_norm=False, head=head,
            cast=(w_in[l], ffn2_in[l], ffn2_out[l]), tm=tm)
        proj, fl, (wa_b, wr_b, wo_b) = _proj(xt, mix_norm[l], w_in_b, seq,
                                             cast=(w_proj_attn[l], w_proj_rec[l], w_out[l]))
        ya = _attn(proj, lambda_q1[l], lambda_k1[l], lambda_q2[l], lambda_k2[l], attn_subln[l],
                   batch, seq, layer=l)
        yr = _hgrn2(proj, fl, rec_lb_raw, rec_gnorm[l], batch, seq, layer=l)
        xt = _merge(xt, ya, yr, proj, wa_b, wr_b, wo_b)
        xt, _ = _ffn(xt, ffn2_norm[l], f2_in_b, f2_in_b, f2_out_b, final_norm,
                     final_norm=(l == depth - 1), tm=tm)
    return xt.reshape(batch, seq, d)
```
